```python
import jax, jax.numpy as jnp
from jax import lax
import numpy as np

D_MODEL = 1024
BATCH = 8
SEQ = 2048
DEPTH = 1

CONV_GROUPS = 16
CONV_GROUP_DIM = 64
D_CONV = CONV_GROUPS * CONV_GROUP_DIM
CONV_WIDTH = 3
N_Q_HEADS = 16
N_KV_HEADS = 2
HEAD_DIM = 64
Q_PER_KV = N_Q_HEADS // N_KV_HEADS
D_ATTN = N_Q_HEADS * HEAD_DIM
D_KV = N_KV_HEADS * HEAD_DIM
WINDOW = 128
BLOCK = 128
D_MIX = D_CONV + D_ATTN
D_IN_PROJ = 4 * D_CONV + 2 * D_ATTN + 2 * D_KV
SPLIT_POINTS = (D_CONV, 2 * D_CONV, 3 * D_CONV, 4 * D_CONV,
                4 * D_CONV + D_ATTN,
                4 * D_CONV + D_ATTN + D_KV,
                4 * D_CONV + D_ATTN + 2 * D_KV)
RMS_EPS = 1e-5

kernel_name = "hybrid_shortconv_swa_sink_alibi_parallel"


def rms_norm(x, gain):
    xf = x.astype(jnp.float32)
    y = xf * lax.rsqrt(jnp.mean(xf * xf, axis=-1, keepdims=True) + RMS_EPS)
    return (y * gain.astype(jnp.float32)).astype(x.dtype)


def alibi_slopes(n_heads):
    return jnp.exp2(-8.0 * jnp.arange(1, n_heads + 1, dtype=jnp.float32) / n_heads)


def short_conv(u, w):
    k = w[:, None, :].astype(u.dtype)
    return lax.conv_general_dilated(
        u, k, window_strides=(1,), padding=[(CONV_WIDTH - 1, 0)],
        dimension_numbers=("NWC", "WIO", "NWC"), feature_group_count=u.shape[-1])


def sliding_window_attention(q, k, v, sinks):
    bsz, seq, _ = q.shape
    nb = seq // BLOCK
    q = q.reshape(bsz, nb, BLOCK, N_KV_HEADS, Q_PER_KV, HEAD_DIM)
    k = k.reshape(bsz, nb, BLOCK, N_KV_HEADS, HEAD_DIM)
    v = v.reshape(bsz, nb, BLOCK, N_KV_HEADS, HEAD_DIM)
    pad = ((0, 0), (1, 0), (0, 0), (0, 0), (0, 0))
    k_band = jnp.concatenate([jnp.pad(k, pad)[:, :-1], k], axis=2)
    v_band = jnp.concatenate([jnp.pad(v, pad)[:, :-1], v], axis=2)
    scores = jnp.einsum("bnqhgd,bnkhd->bnhgqk", q, k_band).astype(jnp.float32)
    scores = scores * (HEAD_DIM ** -0.5)
    qi = jnp.arange(BLOCK)
    kj = jnp.arange(2 * BLOCK)
    dist = BLOCK + qi[:, None] - kj[None, :]
    key_pos = jnp.arange(nb)[:, None] * BLOCK - BLOCK + kj[None, :]
    valid = ((dist >= 0) & (dist < WINDOW))[None, :, :] & (key_pos >= 0)[:, None, :]
    slopes = alibi_slopes(N_Q_HEADS).reshape(N_KV_HEADS, Q_PER_KV)
    alibi = -slopes[:, :, None, None] * dist.astype(jnp.float32)
    scores = jnp.where(valid[None, :, None, None], scores + alibi[None, None], -jnp.inf)
    sink = sinks.astype(jnp.float32).reshape(N_KV_HEADS, Q_PER_KV)[None, None, :, :, None, None]
    m = jnp.maximum(jnp.max(scores, axis=-1, keepdims=True), sink)
    p = jnp.exp(scores - m)
    denom = jnp.sum(p, axis=-1, keepdims=True) + jnp.exp(sink - m)
    probs = (p / denom).astype(v.dtype)
    out = jnp.einsum("bnhgqk,bnkhd->bnqhgd", probs, v_band)
    return out.reshape(bsz, seq, D_ATTN)


def setup_inputs(seed: int = 0) -> dict:
    key = jax.random.key(seed)
    ks = jax.random.split(key, 10)
    f32 = jnp.float32
    x = jax.random.normal(ks[0], (BATCH, SEQ, D_MODEL), f32)
    norm_in = 1.0 + 0.1 * jax.random.normal(ks[1], (DEPTH, D_MODEL), f32)
    w_in = jax.random.normal(ks[2], (DEPTH, D_MODEL, D_IN_PROJ), f32) * D_MODEL ** -0.5
    conv_w = jax.random.normal(ks[3], (DEPTH, CONV_WIDTH, D_CONV), f32) * CONV_WIDTH ** -0.5
    attn_sinks = jax.random.normal(ks[4], (DEPTH, N_Q_HEADS), f32)
    norm_conv_out = 1.0 + 0.1 * jax.random.normal(ks[5], (DEPTH, D_CONV), f32)
    norm_attn_out = 1.0 + 0.1 * jax.random.normal(ks[6], (DEPTH, D_ATTN), f32)
    w_out = jax.random.normal(ks[7], (DEPTH, D_MIX, D_MODEL), f32) * D_MIX ** -0.5
    norm_final = 1.0 + 0.1 * jax.random.normal(ks[8], (D_MODEL,), f32)
    return {"x": x, "norm_in": norm_in, "w_in": w_in, "conv_w": conv_w,
            "attn_sinks": attn_sinks, "norm_conv_out": norm_conv_out,
            "norm_attn_out": norm_attn_out, "w_out": w_out, "norm_final": norm_final}


def reference(x, norm_in, w_in, conv_w, attn_sinks, norm_conv_out, norm_attn_out, w_out, norm_final):
    for layer in range(DEPTH):
        h = rms_norm(x, norm_in[layer])
        proj = jnp.einsum("bsd,de->bse", h, w_in[layer])
        cb, cc, cu, gate_c, q, k, v, gate_a = jnp.split(proj, SPLIT_POINTS, axis=-1)
        conv_y = cb * short_conv(cc * cu, conv_w[layer])
        conv_y = rms_norm(conv_y, norm_conv_out[layer]) * jax.nn.silu(gate_c)
        attn_y = sliding_window_attention(q, k, v, attn_sinks[layer])
        attn_y = rms_norm(attn_y, norm_attn_out[layer]) * jax.nn.silu(gate_a)
        mixed = jnp.concatenate([conv_y, attn_y], axis=-1)
        x = x + jnp.einsum("bse,ed->bsd", mixed, w_out[layer])
    return rms_norm(x, norm_final)
```

```python
import functools

import jax
import jax.numpy as jnp
from jax import lax
from jax.experimental import pallas as pl
from jax.experimental.pallas import tpu as pltpu

D_MODEL = 1024
D_CONV = 1024
CONV_WIDTH = 3
N_Q_HEADS = 16
N_KV_HEADS = 2
HEAD_DIM = 64
Q_PER_KV = N_Q_HEADS // N_KV_HEADS
D_ATTN = N_Q_HEADS * HEAD_DIM
D_KV = N_KV_HEADS * HEAD_DIM
BLOCK = 128
D_MIX = D_CONV + D_ATTN
D_IN_PROJ = 4 * D_CONV + 2 * D_ATTN + 2 * D_KV
RMS_EPS = 1e-5

OFF_CB, OFF_CC, OFF_CU, OFF_GC = 0, D_CONV, 2 * D_CONV, 3 * D_CONV
OFF_Q = 4 * D_CONV
OFF_KV = OFF_Q + D_ATTN
OFF_GA = OFF_KV + 2 * D_KV

LANES = 128
SUBLANES = 8
PAIRS_PER_KV = Q_PER_KV * HEAD_DIM // LANES
SEQ_TILE = 256
VMEM_LIMIT_BYTES = 56 * 1024 * 1024

_NT = (((1,), (1,)), ((), ()))


def _rms(x, gain):
    return x * lax.rsqrt(jnp.mean(x * x, axis=-1, keepdims=True) + RMS_EPS) * gain


def _silu(x):
    return x * (1.0 / (1.0 + jnp.exp(-x)))


def _layer_kernel(x_ref, gin_ref, win_ref, convw_ref, sinks_ref, gconv_ref, gattn_ref,
                  wout_ref, gfin_ref, bias_ref, o_ref, ubuf, kbuf, vbuf, attn_buf):
    ts = x_ref.shape[0]
    j = pl.program_id(1)
    f32, bf16 = jnp.float32, jnp.bfloat16

    @pl.when(j == 0)
    def _():
        ubuf[0:SUBLANES, :] = jnp.zeros((SUBLANES, D_CONV), f32)
        kbuf[:, 0:BLOCK, :] = jnp.zeros((4, BLOCK, LANES), bf16)
        vbuf[:, 0:BLOCK, :] = jnp.zeros((4, BLOCK, LANES), bf16)

    x = x_ref[...]
    h = _rms(x, gin_ref[...]).astype(bf16)

    def proj(off, width):
        return jnp.dot(h, win_ref[:, off:off + width], preferred_element_type=f32)

    u = proj(OFF_CC, D_CONV) * proj(OFF_CU, D_CONV)
    ubuf[SUBLANES:SUBLANES + ts, :] = u
    u1 = ubuf[SUBLANES - 1:SUBLANES - 1 + ts, :]
    u2 = ubuf[SUBLANES - 2:SUBLANES - 2 + ts, :]
    ubuf[0:SUBLANES, :] = ubuf[ts:ts + SUBLANES, :]
    cw = convw_ref[...]
    conv = cw[2:3, :] * u + cw[1:2, :] * u1 + cw[0:1, :] * u2
    y = proj(OFF_CB, D_CONV) * conv
    conv_y = _rms(y, gconv_ref[...]) * _silu(proj(OFF_GC, D_CONV))

    q = (proj(OFF_Q, D_ATTN) * (HEAD_DIM ** -0.5)).astype(bf16)
    kv = proj(OFF_KV, 2 * D_KV)
    lane = lax.broadcasted_iota(jnp.int32, (ts, LANES), 1)
    lo = lane < HEAD_DIM

    def lane_variants(t):
        tr = pltpu.roll(t, HEAD_DIM, axis=1)
        zero = jnp.zeros_like(t)
        return [jnp.where(lo, t, zero), jnp.where(lo, zero, tr),
                jnp.where(lo, tr, zero), jnp.where(lo, zero, t)]

    for i, (kk, vv) in enumerate(zip(lane_variants(kv[:, :D_KV]), lane_variants(kv[:, D_KV:]))):
        kbuf[i, BLOCK:BLOCK + ts, :] = kk.astype(bf16)
        vbuf[i, BLOCK:BLOCK + ts, :] = vv.astype(bf16)

    qi = lax.broadcasted_iota(jnp.int32, (BLOCK, BLOCK), 0)
    kj = lax.broadcasted_iota(jnp.int32, (BLOCK, BLOCK), 1)
    tri = kj <= qi
    neg = jnp.where(j == 0, -jnp.inf, 0.0).astype(f32)

    for qb in range(ts // BLOCK):
        r0 = qb * BLOCK
        for g in range(N_KV_HEADS):
            t0 = g * PAIRS_PER_KV
            qs = jnp.concatenate(
                [q[r0:r0 + BLOCK, (t0 + jp) * LANES:(t0 + jp + 1) * LANES] for jp in range(PAIRS_PER_KV)],
                axis=0)
            kcat = jnp.concatenate([kbuf[2 * g, r0:r0 + 2 * BLOCK, :],
                                    kbuf[2 * g + 1, r0:r0 + 2 * BLOCK, :]], axis=0)
            s_all = lax.dot_general(qs, kcat, _NT, preferred_element_type=f32)
            p_rows = []
            for jp in range(PAIRS_PER_KV):
                p_cols = []
                for par in range(2):
                    head = g * Q_PER_KV + 2 * jp + par
                    c0 = par * 2 * BLOCK
                    s_prev = s_all[jp * BLOCK:(jp + 1) * BLOCK, c0:c0 + BLOCK]
                    s_cur = s_all[jp * BLOCK:(jp + 1) * BLOCK, c0 + BLOCK:c0 + 2 * BLOCK]
                    if qb == 0:
                        s_prev = s_prev + neg
                    s = jnp.where(tri, s_cur, s_prev) + bias_ref[head]
                    sink = sinks_ref[0, head]
                    m = jnp.maximum(jnp.max(s, axis=-1, keepdims=True), sink)
                    p = jnp.exp(s - m)
                    denom = jnp.sum(p, axis=-1, keepdims=True) + jnp.exp(sink - m)
                    pn = p * (1.0 / denom)
                    zero = jnp.zeros_like(pn)
                    p_cols.append(jnp.where(tri, zero, pn).astype(bf16))
                    p_cols.append(jnp.where(tri, pn, zero).astype(bf16))
                p_rows.append(jnp.concatenate(p_cols, axis=1))
            p_all = jnp.concatenate(p_rows, axis=0)
            vcat = jnp.concatenate([vbuf[2 * g, r0:r0 + 2 * BLOCK, :],
                                    vbuf[2 * g + 1, r0:r0 + 2 * BLOCK, :]], axis=0)
            o_all = jnp.dot(p_all, vcat, preferred_element_type=f32)
            for jp in range(PAIRS_PER_KV):
                attn_buf[r0:r0 + BLOCK, (t0 + jp) * LANES:(t0 + jp + 1) * LANES] = \
                    o_all[jp * BLOCK:(jp + 1) * BLOCK, :]

    kbuf[:, 0:BLOCK, :] = kbuf[:, ts:ts + BLOCK, :]
    vbuf[:, 0:BLOCK, :] = vbuf[:, ts:ts + BLOCK, :]

    attn_y = _rms(attn_buf[...], gattn_ref[...]) * _silu(proj(OFF_GA, D_ATTN))

    mixed = jnp.concatenate([conv_y, attn_y], axis=-1).astype(bf16)
    xr = x + jnp.dot(mixed, wout_ref[...], preferred_element_type=f32)
    o_ref[...] = _rms(xr, gfin_ref[...])


def _alibi_bias():
    slopes = jnp.exp2(-8.0 * jnp.arange(1, N_Q_HEADS + 1, dtype=jnp.float32) / N_Q_HEADS)
    qi = jnp.arange(BLOCK)[:, None]
    kj = jnp.arange(BLOCK)[None, :]
    dist = jnp.where(kj <= qi, qi - kj, BLOCK + qi - kj).astype(jnp.float32)
    return -slopes[:, None, None] * dist[None]


@jax.jit
def kernel(x, norm_in, w_in, conv_w, attn_sinks, norm_conv_out, norm_attn_out, w_out, norm_final):
    bsz, seq, d = x.shape
    assert d == D_MODEL and seq % SEQ_TILE == 0 and SEQ_TILE % BLOCK == 0
    assert w_in.shape == (1, D_MODEL, D_IN_PROJ) and w_out.shape == (1, D_MIX, D_MODEL)
    ts = SEQ_TILE
    const = lambda shape: pl.BlockSpec(shape, lambda b, j: (0,) * len(shape),
                                       pipeline_mode=pl.Buffered(1))
    return pl.pallas_call(
        _layer_kernel,
        grid=(bsz, seq // ts),
        in_specs=[
            pl.BlockSpec((None, ts, D_MODEL), lambda b, j: (b, j, 0)),
            const((1, D_MODEL)),
            const((D_MODEL, D_IN_PROJ)),
            const((CONV_WIDTH, D_CONV)),
            pl.BlockSpec(memory_space=pltpu.SMEM),
            const((1, D_CONV)),
            const((1, D_ATTN)),
            const((D_MIX, D_MODEL)),
            const((1, D_MODEL)),
            const((N_Q_HEADS, BLOCK, BLOCK)),
        ],
        out_specs=pl.BlockSpec((None, ts, D_MODEL), lambda b, j: (b, j, 0)),
        out_shape=jax.ShapeDtypeStruct(x.shape, x.dtype),
        scratch_shapes=[
            pltpu.VMEM((SUBLANES + ts, D_CONV), jnp.float32),
            pltpu.VMEM((4, BLOCK + ts, LANES), jnp.bfloat16),
            pltpu.VMEM((4, BLOCK + ts, LANES), jnp.bfloat16),
            pltpu.VMEM((ts, D_ATTN), jnp.float32),
        ],
        compiler_params=pltpu.CompilerParams(
            dimension_semantics=("arbitrary", "arbitrary"),
            vmem_limit_bytes=VMEM_LIMIT_BYTES),
        name="hybrid_layer",
    )(x, norm_in, w_in[0].astype(jnp.bfloat16), conv_w[0], attn_sinks, norm_conv_out, norm_attn_out,
      w_out[0].astype(jnp.bfloat16), norm_final[None, :], _alibi_bias())
```

```python
import functools

import jax
import jax.numpy as jnp
from jax import lax
from jax.experimental import pallas as pl
from jax.experimental.pallas import tpu as pltpu

D_MODEL = 1024
D_CONV = 1024
CONV_WIDTH = 3
N_Q_HEADS = 16
N_KV_HEADS = 2
HEAD_DIM = 64
Q_PER_KV = N_Q_HEADS // N_KV_HEADS
D_ATTN = N_Q_HEADS * HEAD_DIM
D_KV = N_KV_HEADS * HEAD_DIM
BLOCK = 128
D_MIX = D_CONV + D_ATTN
D_IN_PROJ = 4 * D_CONV + 2 * D_ATTN + 2 * D_KV
RMS_EPS = 1e-5

OFF_CB, OFF_CC, OFF_CU, OFF_GC = 0, D_CONV, 2 * D_CONV, 3 * D_CONV
OFF_Q = 4 * D_CONV
OFF_KV = OFF_Q + D_ATTN
OFF_GA = OFF_KV + 2 * D_KV

LANES = 128
SUBLANES = 8
PAIRS_PER_KV = Q_PER_KV * HEAD_DIM // LANES
SEQ_TILE = 512
VMEM_LIMIT_BYTES = 56 * 1024 * 1024

_NT = (((1,), (1,)), ((), ()))


def _rms(x, gain):
    return x * lax.rsqrt(jnp.mean(x * x, axis=-1, keepdims=True) + RMS_EPS) * gain


def _silu(x):
    return x * (1.0 / (1.0 + jnp.exp(-x)))


def _layer_kernel(x_ref, gin_ref, win_ref, convw_ref, sinks_ref, gconv_ref, gattn_ref,
                  wout_ref, gfin_ref, bias_ref, o_ref, ubuf, kbuf, vbuf, attn_buf):
    ts = x_ref.shape[0]
    j = pl.program_id(1)
    f32, bf16 = jnp.float32, jnp.bfloat16

    @pl.when(j == 0)
    def _():
        ubuf[0:SUBLANES, :] = jnp.zeros((SUBLANES, D_CONV), f32)
        kbuf[:, 0:BLOCK, :] = jnp.zeros((4, BLOCK, LANES), bf16)
        vbuf[:, 0:BLOCK, :] = jnp.zeros((4, BLOCK, LANES), bf16)

    x = x_ref[...]
    h = _rms(x, gin_ref[...]).astype(bf16)

    def proj(off, width):
        return jnp.dot(h, win_ref[:, off:off + width], preferred_element_type=f32)

    u = proj(OFF_CC, D_CONV) * proj(OFF_CU, D_CONV)
    ubuf[SUBLANES:SUBLANES + ts, :] = u
    u1 = ubuf[SUBLANES - 1:SUBLANES - 1 + ts, :]
    u2 = ubuf[SUBLANES - 2:SUBLANES - 2 + ts, :]
    ubuf[0:SUBLANES, :] = ubuf[ts:ts + SUBLANES, :]
    cw = convw_ref[...]
    conv = cw[2:3, :] * u + cw[1:2, :] * u1 + cw[0:1, :] * u2
    y = proj(OFF_CB, D_CONV) * conv
    conv_y = _rms(y, gconv_ref[...]) * _silu(proj(OFF_GC, D_CONV))

    q = (proj(OFF_Q, D_ATTN) * (HEAD_DIM ** -0.5)).astype(bf16)
    kv = proj(OFF_KV, 2 * D_KV)
    lane = lax.broadcasted_iota(jnp.int32, (ts, LANES), 1)
    lo = lane < HEAD_DIM

    def lane_variants(t):
        tr = pltpu.roll(t, HEAD_DIM, axis=1)
        zero = jnp.zeros_like(t)
        return [jnp.where(lo, t, zero), jnp.where(lo, zero, tr),
                jnp.where(lo, tr, zero), jnp.where(lo, zero, t)]

    for i, (kk, vv) in enumerate(zip(lane_variants(kv[:, :D_KV]), lane_variants(kv[:, D_KV:]))):
        kbuf[i, BLOCK:BLOCK + ts, :] = kk.astype(bf16)
        vbuf[i, BLOCK:BLOCK + ts, :] = vv.astype(bf16)

    qi = lax.broadcasted_iota(jnp.int32, (BLOCK, BLOCK), 0)
    kj = lax.broadcasted_iota(jnp.int32, (BLOCK, BLOCK), 1)
    tri = kj <= qi
    neg = jnp.where(j == 0, -jnp.inf, 0.0).astype(f32)

    for qb in range(ts // BLOCK):
        r0 = qb * BLOCK
        for g in range(N_KV_HEADS):
            t0 = g * PAIRS_PER_KV
            qs = jnp.concatenate(
                [q[r0:r0 + BLOCK, (t0 + jp) * LANES:(t0 + jp + 1) * LANES] for jp in range(PAIRS_PER_KV)],
                axis=0)
            kcat = jnp.concatenate([kbuf[2 * g, r0:r0 + 2 * BLOCK, :],
                                    kbuf[2 * g + 1, r0:r0 + 2 * BLOCK, :]], axis=0)
            s_all = lax.dot_general(qs, kcat, _NT, preferred_element_type=f32)
            p_rows = []
            for jp in range(PAIRS_PER_KV):
                p_cols = []
                for par in range(2):
                    head = g * Q_PER_KV + 2 * jp + par
                    c0 = par * 2 * BLOCK
                    s_prev = s_all[jp * BLOCK:(jp + 1) * BLOCK, c0:c0 + BLOCK]
                    s_cur = s_all[jp * BLOCK:(jp + 1) * BLOCK, c0 + BLOCK:c0 + 2 * BLOCK]
                    if qb == 0:
                        s_prev = s_prev + neg
                    s = jnp.where(tri, s_cur, s_prev) + bias_ref[head]
                    sink = sinks_ref[0, head]
                    m = jnp.maximum(jnp.max(s, axis=-1, keepdims=True), sink)
                    p = jnp.exp(s - m)
                    denom = jnp.sum(p, axis=-1, keepdims=True) + jnp.exp(sink - m)
                    pn = p * (1.0 / denom)
                    zero = jnp.zeros_like(pn)
                    p_cols.append(jnp.where(tri, zero, pn).astype(bf16))
                    p_cols.append(jnp.where(tri, pn, zero).astype(bf16))
                p_rows.append(jnp.concatenate(p_cols, axis=1))
            p_all = jnp.concatenate(p_rows, axis=0)
            vcat = jnp.concatenate([vbuf[2 * g, r0:r0 + 2 * BLOCK, :],
                                    vbuf[2 * g + 1, r0:r0 + 2 * BLOCK, :]], axis=0)
            o_all = jnp.dot(p_all, vcat, preferred_element_type=f32)
            for jp in range(PAIRS_PER_KV):
                attn_buf[r0:r0 + BLOCK, (t0 + jp) * LANES:(t0 + jp + 1) * LANES] = \
                    o_all[jp * BLOCK:(jp + 1) * BLOCK, :]

    kbuf[:, 0:BLOCK, :] = kbuf[:, ts:ts + BLOCK, :]
    vbuf[:, 0:BLOCK, :] = vbuf[:, ts:ts + BLOCK, :]

    attn_y = _rms(attn_buf[...], gattn_ref[...]) * _silu(proj(OFF_GA, D_ATTN))

    mixed = jnp.concatenate([conv_y, attn_y], axis=-1).astype(bf16)
    xr = x + jnp.dot(mixed, wout_ref[...], preferred_element_type=f32)
    o_ref[...] = _rms(xr, gfin_ref[...])


def _alibi_bias():
    slopes = jnp.exp2(-8.0 * jnp.arange(1, N_Q_HEADS + 1, dtype=jnp.float32) / N_Q_HEADS)
    qi = jnp.arange(BLOCK)[:, None]
    kj = jnp.arange(BLOCK)[None, :]
    dist = jnp.where(kj <= qi, qi - kj, BLOCK + qi - kj).astype(jnp.float32)
    return -slopes[:, None, None] * dist[None]


@jax.jit
def kernel(x, norm_in, w_in, conv_w, attn_sinks, norm_conv_out, norm_attn_out, w_out, norm_final):
    bsz, seq, d = x.shape
    assert d == D_MODEL and seq % SEQ_TILE == 0 and SEQ_TILE % BLOCK == 0
    assert w_in.shape == (1, D_MODEL, D_IN_PROJ) and w_out.shape == (1, D_MIX, D_MODEL)
    ts = SEQ_TILE
    const = lambda shape: pl.BlockSpec(shape, lambda b, j: (0,) * len(shape),
                                       pipeline_mode=pl.Buffered(1))
    return pl.pallas_call(
        _layer_kernel,
        grid=(bsz, seq // ts),
        in_specs=[
            pl.BlockSpec((None, ts, D_MODEL), lambda b, j: (b, j, 0)),
            const((1, D_MODEL)),
            const((D_MODEL, D_IN_PROJ)),
            const((CONV_WIDTH, D_CONV)),
            pl.BlockSpec(memory_space=pltpu.SMEM),
            const((1, D_CONV)),
            const((1, D_ATTN)),
            const((D_MIX, D_MODEL)),
            const((1, D_MODEL)),
            const((N_Q_HEADS, BLOCK, BLOCK)),
        ],
        out_specs=pl.BlockSpec((None, ts, D_MODEL), lambda b, j: (b, j, 0)),
        out_shape=jax.ShapeDtypeStruct(x.shape, x.dtype),
        scratch_shapes=[
            pltpu.VMEM((SUBLANES + ts, D_CONV), jnp.float32),
            pltpu.VMEM((4, BLOCK + ts, LANES), jnp.bfloat16),
            pltpu.VMEM((4, BLOCK + ts, LANES), jnp.bfloat16),
            pltpu.VMEM((ts, D_ATTN), jnp.float32),
        ],
        compiler_params=pltpu.CompilerParams(
            dimension_semantics=("arbitrary", "arbitrary"),
            vmem_limit_bytes=VMEM_LIMIT_BYTES),
        name="hybrid_layer",
    )(x, norm_in, w_in[0].astype(jnp.bfloat16), conv_w[0], attn_sinks, norm_conv_out, norm_attn_out,
      w_out[0].astype(jnp.bfloat16), norm_final[None, :], _alibi_bias())
```

```python
import jax
import jax.numpy as jnp
from jax import lax
from jax.experimental import pallas as pl
from jax.experimental.pallas import tpu as pltpu

D_MODEL = 1024
D_CONV = 1024
CONV_WIDTH = 3
N_Q_HEADS = 16
N_KV_HEADS = 2
HEAD_DIM = 64
Q_PER_KV = N_Q_HEADS // N_KV_HEADS
D_ATTN = N_Q_HEADS * HEAD_DIM
D_KV = N_KV_HEADS * HEAD_DIM
BLOCK = 128
D_MIX = D_CONV + D_ATTN
D_IN_PROJ = 4 * D_CONV + 2 * D_ATTN + 2 * D_KV
RMS_EPS = 1e-5

OFF_CB, OFF_CC, OFF_CU, OFF_GC = 0, D_CONV, 2 * D_CONV, 3 * D_CONV
OFF_Q = 4 * D_CONV
OFF_KV = OFF_Q + D_ATTN
OFF_GA = OFF_KV + 2 * D_KV

LANES = 128
SUBLANES = 8
PAIRS_PER_KV = Q_PER_KV * HEAD_DIM // LANES
SEQ_TILE = 512
CONV_CHUNK = 256
VMEM_LIMIT_BYTES = 56 * 1024 * 1024

_NT = (((1,), (1,)), ((), ()))


def _rms(x, gain):
    return x * lax.rsqrt(jnp.mean(x * x, axis=-1, keepdims=True) + RMS_EPS) * gain


def _silu(x):
    return x * (1.0 / (1.0 + jnp.exp(-x)))


def _layer_kernel(x_ref, gin_ref, win_ref, convw_ref, sinks_ref, gconv_ref, gattn_ref,
                  wout_ref, gfin_ref, bias_ref, o_ref, ubuf, kbuf, vtbuf, qbuf, attn_buf, ybuf, sgabuf):
    ts = x_ref.shape[0]
    j = pl.program_id(1)
    f32, bf16 = jnp.float32, jnp.bfloat16

    @pl.when(j == 0)
    def _():
        ubuf[0:SUBLANES, :] = jnp.zeros((SUBLANES, D_CONV), f32)
        kbuf[:, 0:BLOCK, :] = jnp.zeros((4, BLOCK, LANES), bf16)
        vtbuf[:, :, 0:BLOCK] = jnp.zeros((4, LANES, BLOCK), bf16)

    x = x_ref[...]
    h = _rms(x, gin_ref[...]).astype(bf16)

    def proj(off, width):
        return jnp.dot(h, win_ref[:, off:off + width], preferred_element_type=f32)

    qbuf[...] = (proj(OFF_Q, D_ATTN) * (HEAD_DIM ** -0.5)).astype(bf16)
    kv = proj(OFF_KV, 2 * D_KV)
    k = kv[:, :D_KV]
    lo = lax.broadcasted_iota(jnp.int32, (ts, LANES), 1) < HEAD_DIM
    k_swapped = pltpu.roll(k, HEAD_DIM, axis=1)
    zero = jnp.zeros_like(k)
    for i, kk in enumerate([jnp.where(lo, k, zero), jnp.where(lo, zero, k_swapped),
                            jnp.where(lo, k_swapped, zero), jnp.where(lo, zero, k)]):
        kbuf[i, BLOCK:BLOCK + ts, :] = kk.astype(bf16)
    vt = kv[:, D_KV:].T.astype(bf16)
    vt0, vt1 = vt[:HEAD_DIM, :], vt[HEAD_DIM:, :]
    zrows = jnp.zeros_like(vt0)
    for i, vv in enumerate([(vt0, zrows), (zrows, vt0), (vt1, zrows), (zrows, vt1)]):
        vtbuf[i, :, BLOCK:BLOCK + ts] = jnp.concatenate(vv, axis=0)

    ki = lax.broadcasted_iota(jnp.int32, (BLOCK, BLOCK), 0)
    qj = lax.broadcasted_iota(jnp.int32, (BLOCK, BLOCK), 1)
    tri = ki <= qj
    neg = jnp.where(j == 0, -jnp.inf, 0.0).astype(f32)

    chunks = [(qb, g) for qb in range(ts // BLOCK) for g in range(N_KV_HEADS)]

    def scores(chunk):
        qb, g = chunk
        r0, t0 = qb * BLOCK, g * PAIRS_PER_KV
        qs = jnp.concatenate(
            [qbuf[r0:r0 + BLOCK, (t0 + jp) * LANES:(t0 + jp + 1) * LANES] for jp in range(PAIRS_PER_KV)],
            axis=0)
        kcat = jnp.concatenate([kbuf[2 * g, r0:r0 + 2 * BLOCK, :],
                                kbuf[2 * g + 1, r0:r0 + 2 * BLOCK, :]], axis=0)
        return lax.dot_general(kcat, qs, _NT, preferred_element_type=f32)

    def softmax_pv(chunk, s_all):
        qb, g = chunk
        r0, t0 = qb * BLOCK, g * PAIRS_PER_KV
        p_blocks = [[None] * PAIRS_PER_KV for _ in range(4)]
        for jp in range(PAIRS_PER_KV):
            for par in range(2):
                head = g * Q_PER_KV + 2 * jp + par
                k0 = par * 2 * BLOCK
                cols = slice(jp * BLOCK, (jp + 1) * BLOCK)
                s_prev = s_all[k0:k0 + BLOCK, cols]
                s_cur = s_all[k0 + BLOCK:k0 + 2 * BLOCK, cols]
                if qb == 0:
                    s_prev = s_prev + neg
                s = jnp.where(tri, s_cur, s_prev) + bias_ref[head]
                sink = sinks_ref[0, head]
                m = jnp.maximum(jnp.max(s, axis=0, keepdims=True), sink)
                p = jnp.exp(s - m)
                denom = jnp.sum(p, axis=0, keepdims=True) + jnp.exp(sink - m)
                pn = p * (1.0 / denom)
                zero_p = jnp.zeros_like(pn)
                p_blocks[2 * par][jp] = jnp.where(tri, zero_p, pn).astype(bf16)
                p_blocks[2 * par + 1][jp] = jnp.where(tri, pn, zero_p).astype(bf16)
        p_all = jnp.concatenate([jnp.concatenate(row, axis=1) for row in p_blocks], axis=0)
        vtcat = jnp.concatenate([vtbuf[2 * g, :, r0:r0 + 2 * BLOCK],
                                 vtbuf[2 * g + 1, :, r0:r0 + 2 * BLOCK]], axis=1)
        o_all = jnp.dot(vtcat, p_all, preferred_element_type=f32)
        for jp in range(PAIRS_PER_KV):
            attn_buf[r0:r0 + BLOCK, (t0 + jp) * LANES:(t0 + jp + 1) * LANES] = \
                o_all[:, jp * BLOCK:(jp + 1) * BLOCK].T

    def conv_proj(c0):
        return [proj(off + c0, CONV_CHUNK) for off in (OFF_CC, OFF_CU, OFF_CB, OFF_GC)]

    def conv_elementwise(c0, projected):
        cc, cu, cb, gc = projected
        cs = slice(c0, c0 + CONV_CHUNK)
        u = cc * cu
        ubuf[SUBLANES:SUBLANES + ts, cs] = u
        u1 = ubuf[SUBLANES - 1:SUBLANES - 1 + ts, cs]
        u2 = ubuf[SUBLANES - 2:SUBLANES - 2 + ts, cs]
        ubuf[0:SUBLANES, cs] = ubuf[ts:ts + SUBLANES, cs]
        y = cb * (convw_ref[2:3, cs] * u + convw_ref[1:2, cs] * u1 + convw_ref[0:1, cs] * u2)
        ybuf[:, cs] = y * _silu(gc)
        return jnp.sum(y * y, axis=-1, keepdims=True)

    n_conv = D_CONV // CONV_CHUNK
    assert len(chunks) == 2 * n_conv
    s_ahead = [scores(chunks[0]), scores(chunks[1])]
    sgabuf[...] = _silu(proj(OFF_GA, D_ATTN))
    ssq = jnp.zeros((ts, 1), f32)
    pending = None
    for c in range(n_conv):
        for i in (2 * c, 2 * c + 1):
            softmax_pv(chunks[i], s_ahead.pop(0))
            if i + 2 < len(chunks):
                s_ahead.append(scores(chunks[i + 2]))
        if pending is not None:
            ssq = ssq + conv_elementwise(*pending)
        pending = (c * CONV_CHUNK, conv_proj(c * CONV_CHUNK))

    kbuf[:, 0:BLOCK, :] = kbuf[:, ts:ts + BLOCK, :]
    vtbuf[:, :, 0:BLOCK] = vtbuf[:, :, ts:ts + BLOCK]

    attn_y = (_rms(attn_buf[...], gattn_ref[...]) * sgabuf[...]).astype(bf16)
    o = jnp.dot(attn_y, wout_ref[D_CONV:D_MIX, :], preferred_element_type=f32)
    ssq = ssq + conv_elementwise(*pending)
    conv_y = (ybuf[...] * lax.rsqrt(ssq * (1.0 / D_CONV) + RMS_EPS) * gconv_ref[...]).astype(bf16)
    o = o + jnp.dot(conv_y, wout_ref[0:D_CONV, :], preferred_element_type=f32)
    o_ref[...] = _rms(x + o, gfin_ref[...])


def _alibi_bias():
    slopes = jnp.exp2(-8.0 * jnp.arange(1, N_Q_HEADS + 1, dtype=jnp.float32) / N_Q_HEADS)
    ki = jnp.arange(BLOCK)[:, None]
    qj = jnp.arange(BLOCK)[None, :]
    dist = jnp.where(ki <= qj, qj - ki, BLOCK + qj - ki).astype(jnp.float32)
    return -slopes[:, None, None] * dist[None]


@jax.jit
def kernel(x, norm_in, w_in, conv_w, attn_sinks, norm_conv_out, norm_attn_out, w_out, norm_final):
    bsz, seq, d = x.shape
    assert d == D_MODEL and seq % SEQ_TILE == 0 and SEQ_TILE % BLOCK == 0
    assert w_in.shape == (1, D_MODEL, D_IN_PROJ) and w_out.shape == (1, D_MIX, D_MODEL)
    ts = SEQ_TILE
    const = lambda shape: pl.BlockSpec(shape, lambda b, j: (0,) * len(shape),
                                       pipeline_mode=pl.Buffered(1))
    return pl.pallas_call(
        _layer_kernel,
        grid=(bsz, seq // ts),
        in_specs=[
            pl.BlockSpec((None, ts, D_MODEL), lambda b, j: (b, j, 0)),
            const((1, D_MODEL)),
            const((D_MODEL, D_IN_PROJ)),
            const((CONV_WIDTH, D_CONV)),
            pl.BlockSpec(memory_space=pltpu.SMEM),
            const((1, D_CONV)),
            const((1, D_ATTN)),
            const((D_MIX, D_MODEL)),
            const((1, D_MODEL)),
            const((N_Q_HEADS, BLOCK, BLOCK)),
        ],
        out_specs=pl.BlockSpec((None, ts, D_MODEL), lambda b, j: (b, j, 0)),
        out_shape=jax.ShapeDtypeStruct(x.shape, x.dtype),
        scratch_shapes=[
            pltpu.VMEM((SUBLANES + ts, D_CONV), jnp.float32),
            pltpu.VMEM((4, BLOCK + ts, LANES), jnp.bfloat16),
            pltpu.VMEM((4, LANES, BLOCK + ts), jnp.bfloat16),
            pltpu.VMEM((ts, D_ATTN), jnp.bfloat16),
            pltpu.VMEM((ts, D_ATTN), jnp.float32),
            pltpu.VMEM((ts, D_CONV), jnp.float32),
            pltpu.VMEM((ts, D_ATTN), jnp.float32),
        ],
        compiler_params=pltpu.CompilerParams(
            dimension_semantics=("arbitrary", "arbitrary"),
            vmem_limit_bytes=VMEM_LIMIT_BYTES),
        name="hybrid_layer",
    )(x, norm_in, w_in[0].astype(jnp.bfloat16), conv_w[0], attn_sinks, norm_conv_out, norm_attn_out,
      w_out[0].astype(jnp.bfloat16), norm_final[None, :], _alibi_bias())
```

```python
import jax
import jax.numpy as jnp
from jax import lax
from jax.experimental import pallas as pl
from jax.experimental.pallas import tpu as pltpu

D_MODEL = 1024
D_CONV = 1024
CONV_WIDTH = 3
N_Q_HEADS = 16
N_KV_HEADS = 2
HEAD_DIM = 64
Q_PER_KV = N_Q_HEADS // N_KV_HEADS
D_ATTN = N_Q_HEADS * HEAD_DIM
D_KV = N_KV_HEADS * HEAD_DIM
BLOCK = 128
D_MIX = D_CONV + D_ATTN
D_IN_PROJ = 4 * D_CONV + 2 * D_ATTN + 2 * D_KV
RMS_EPS = 1e-5

OFF_CB, OFF_CC, OFF_CU, OFF_GC = 0, D_CONV, 2 * D_CONV, 3 * D_CONV
OFF_Q = 4 * D_CONV
OFF_KV = OFF_Q + D_ATTN
OFF_GA = OFF_KV + 2 * D_KV

LANES = 128
SUBLANES = 8
PAIRS_PER_KV = Q_PER_KV * HEAD_DIM // LANES
SEQ_TILE = 512
CONV_CHUNK = 256
VMEM_LIMIT_BYTES = 56 * 1024 * 1024

_NT = (((1,), (1,)), ((), ()))


def _rms(x, gain):
    return x * lax.rsqrt(jnp.mean(x * x, axis=-1, keepdims=True) + RMS_EPS) * gain


def _silu(x):
    return x * (1.0 / (1.0 + jnp.exp(-x)))


def _layer_kernel(x_ref, gin_ref, win_ref, convw_ref, sinks_ref, gconv_ref, gattn_ref,
                  wout_ref, gfin_ref, bias_ref, o_ref, ubuf, kbuf, vtbuf, qbuf, attn_buf, cbuf, sgabuf):
    ts = x_ref.shape[0]
    j = pl.program_id(1)
    f32, bf16 = jnp.float32, jnp.bfloat16

    @pl.when(j == 0)
    def _():
        ubuf[0:SUBLANES, :] = jnp.zeros((SUBLANES, D_CONV), f32)
        kbuf[:, 0:BLOCK, :] = jnp.zeros((4, BLOCK, LANES), bf16)
        vtbuf[:, :, 0:BLOCK] = jnp.zeros((4, LANES, BLOCK), bf16)

    x = x_ref[...]
    h = _rms(x, gin_ref[...]).astype(bf16)

    def proj(off, width):
        return jnp.dot(h, win_ref[:, off:off + width], preferred_element_type=f32)

    qbuf[...] = (proj(OFF_Q, D_ATTN) * (HEAD_DIM ** -0.5)).astype(bf16)
    kv = proj(OFF_KV, 2 * D_KV)
    k = kv[:, :D_KV]
    lo = lax.broadcasted_iota(jnp.int32, (ts, LANES), 1) < HEAD_DIM
    k_swapped = pltpu.roll(k, HEAD_DIM, axis=1)
    zero = jnp.zeros_like(k)
    for i, kk in enumerate([jnp.where(lo, k, zero), jnp.where(lo, zero, k_swapped),
                            jnp.where(lo, k_swapped, zero), jnp.where(lo, zero, k)]):
        kbuf[i, BLOCK:BLOCK + ts, :] = kk.astype(bf16)
    vt = kv[:, D_KV:].T.astype(bf16)
    vt0, vt1 = vt[:HEAD_DIM, :], vt[HEAD_DIM:, :]
    zrows = jnp.zeros_like(vt0)
    for i, vv in enumerate([(vt0, zrows), (zrows, vt0), (vt1, zrows), (zrows, vt1)]):
        vtbuf[i, :, BLOCK:BLOCK + ts] = jnp.concatenate(vv, axis=0)

    ki = lax.broadcasted_iota(jnp.int32, (BLOCK, BLOCK), 0)
    qj = lax.broadcasted_iota(jnp.int32, (BLOCK, BLOCK), 1)
    tri = ki <= qj
    neg = jnp.where(j == 0, -jnp.inf, 0.0).astype(f32)

    chunks = [(qb, g) for qb in range(ts // BLOCK) for g in range(N_KV_HEADS)]

    def scores(chunk):
        qb, g = chunk
        r0, t0 = qb * BLOCK, g * PAIRS_PER_KV
        qs = jnp.concatenate(
            [qbuf[r0:r0 + BLOCK, (t0 + jp) * LANES:(t0 + jp + 1) * LANES] for jp in range(PAIRS_PER_KV)],
            axis=0)
        kcat = jnp.concatenate([kbuf[2 * g, r0:r0 + 2 * BLOCK, :],
                                kbuf[2 * g + 1, r0:r0 + 2 * BLOCK, :]], axis=0)
        return lax.dot_general(kcat, qs, _NT, preferred_element_type=f32)

    def softmax(chunk, s_all):
        qb, g = chunk
        p_blocks = [[None] * PAIRS_PER_KV for _ in range(4)]
        for jp in range(PAIRS_PER_KV):
            for par in range(2):
                head = g * Q_PER_KV + 2 * jp + par
                k0 = par * 2 * BLOCK
                cols = slice(jp * BLOCK, (jp + 1) * BLOCK)
                s_prev = s_all[k0:k0 + BLOCK, cols]
                s_cur = s_all[k0 + BLOCK:k0 + 2 * BLOCK, cols]
                if qb == 0:
                    s_prev = s_prev + neg
                s = jnp.where(tri, s_cur, s_prev) + bias_ref[head]
                sink = sinks_ref[0, head]
                m = jnp.maximum(jnp.max(s, axis=0, keepdims=True), sink)
                p = jnp.exp(s - m)
                denom = jnp.sum(p, axis=0, keepdims=True) + jnp.exp(sink - m)
                pn = p * (1.0 / denom)
                zero_p = jnp.zeros_like(pn)
                p_blocks[2 * par][jp] = jnp.where(tri, zero_p, pn).astype(bf16)
                p_blocks[2 * par + 1][jp] = jnp.where(tri, pn, zero_p).astype(bf16)
        return jnp.concatenate([jnp.concatenate(row, axis=1) for row in p_blocks], axis=0)

    def weighted_values(chunk, p_all):
        qb, g = chunk
        r0, t0 = qb * BLOCK, g * PAIRS_PER_KV
        vtcat = jnp.concatenate([vtbuf[2 * g, :, r0:r0 + 2 * BLOCK],
                                 vtbuf[2 * g + 1, :, r0:r0 + 2 * BLOCK]], axis=1)
        o_all = jnp.dot(vtcat, p_all, preferred_element_type=f32)
        for jp in range(PAIRS_PER_KV):
            attn_buf[r0:r0 + BLOCK, (t0 + jp) * LANES:(t0 + jp + 1) * LANES] = \
                o_all[:, jp * BLOCK:(jp + 1) * BLOCK].T

    def conv_elementwise(c, cc, cu, cb, gc):
        cs = slice(c * CONV_CHUNK, (c + 1) * CONV_CHUNK)
        u = cc * cu
        ubuf[SUBLANES:SUBLANES + ts, cs] = u
        u1 = ubuf[SUBLANES - 1:SUBLANES - 1 + ts, cs]
        u2 = ubuf[SUBLANES - 2:SUBLANES - 2 + ts, cs]
        ubuf[0:SUBLANES, cs] = ubuf[ts:ts + SUBLANES, cs]
        y = cb * (convw_ref[2:3, cs] * u + convw_ref[1:2, cs] * u1 + convw_ref[0:1, cs] * u2)
        cbuf[:, cs] = (y * gconv_ref[:, cs] * _silu(gc)).astype(bf16)
        return jnp.sum(y * y, axis=-1, keepdims=True)

    n_conv = D_CONV // CONV_CHUNK
    assert len(chunks) == 2 * n_conv
    sgabuf[...] = _silu(proj(OFF_GA, D_ATTN))
    s_ahead = [scores(chunks[0]), scores(chunks[1])]
    ssq_c = jnp.zeros((ts, 1), f32)
    previous = None
    for c in range(n_conv):
        projected = []
        for i, off in ((2 * c, OFF_CC), (2 * c + 1, OFF_CU)):
            projected.append(proj(off + c * CONV_CHUNK, CONV_CHUNK))
            weighted_values(chunks[i], softmax(chunks[i], s_ahead.pop(0)))
            if i + 2 < len(chunks):
                s_ahead.append(scores(chunks[i + 2]))
        projected.append(proj(OFF_CB + c * CONV_CHUNK, CONV_CHUNK))
        if previous is not None:
            ssq_c = ssq_c + conv_elementwise(c - 1, *previous)
        projected.append(proj(OFF_GC + c * CONV_CHUNK, CONV_CHUNK))
        previous = projected

    kbuf[:, 0:BLOCK, :] = kbuf[:, ts:ts + BLOCK, :]
    vtbuf[:, :, 0:BLOCK] = vtbuf[:, :, ts:ts + BLOCK]

    attn = attn_buf[...]
    attn_y = (attn * gattn_ref[...] * sgabuf[...]).astype(bf16)
    split = (n_conv - 1) * CONV_CHUNK
    o_c = jnp.dot(cbuf[:, 0:split], wout_ref[0:split, :], preferred_element_type=f32)
    ssq_c = ssq_c + conv_elementwise(n_conv - 1, *previous)
    o_a = jnp.dot(attn_y, wout_ref[D_CONV:D_MIX, :], preferred_element_type=f32)
    o_c = o_c + jnp.dot(cbuf[:, split:D_CONV], wout_ref[split:D_CONV, :], preferred_element_type=f32)
    r_a = lax.rsqrt(jnp.mean(attn * attn, axis=-1, keepdims=True) + RMS_EPS)
    r_c = lax.rsqrt(ssq_c * (1.0 / D_CONV) + RMS_EPS)
    o_ref[...] = _rms(x + r_a * o_a + r_c * o_c, gfin_ref[...])


def _alibi_bias():
    slopes = jnp.exp2(-8.0 * jnp.arange(1, N_Q_HEADS + 1, dtype=jnp.float32) / N_Q_HEADS)
    ki = jnp.arange(BLOCK)[:, None]
    qj = jnp.arange(BLOCK)[None, :]
    dist = jnp.where(ki <= qj, qj - ki, BLOCK + qj - ki).astype(jnp.float32)
    return -slopes[:, None, None] * dist[None]


@jax.jit
def kernel(x, norm_in, w_in, conv_w, attn_sinks, norm_conv_out, norm_attn_out, w_out, norm_final):
    bsz, seq, d = x.shape
    assert d == D_MODEL and seq % SEQ_TILE == 0 and SEQ_TILE % BLOCK == 0
    assert w_in.shape == (1, D_MODEL, D_IN_PROJ) and w_out.shape == (1, D_MIX, D_MODEL)
    ts = SEQ_TILE
    const = lambda shape: pl.BlockSpec(shape, lambda b, j: (0,) * len(shape),
                                       pipeline_mode=pl.Buffered(1))
    return pl.pallas_call(
        _layer_kernel,
        grid=(bsz, seq // ts),
        in_specs=[
            pl.BlockSpec((None, ts, D_MODEL), lambda b, j: (b, j, 0)),
            const((1, D_MODEL)),
            const((D_MODEL, D_IN_PROJ)),
            const((CONV_WIDTH, D_CONV)),
            pl.BlockSpec(memory_space=pltpu.SMEM),
            const((1, D_CONV)),
            const((1, D_ATTN)),
            const((D_MIX, D_MODEL)),
            const((1, D_MODEL)),
            const((N_Q_HEADS, BLOCK, BLOCK)),
        ],
        out_specs=pl.BlockSpec((None, ts, D_MODEL), lambda b, j: (b, j, 0)),
        out_shape=jax.ShapeDtypeStruct(x.shape, x.dtype),
        scratch_shapes=[
            pltpu.VMEM((SUBLANES + ts, D_CONV), jnp.float32),
            pltpu.VMEM((4, BLOCK + ts, LANES), jnp.bfloat16),
            pltpu.VMEM((4, LANES, BLOCK + ts), jnp.bfloat16),
            pltpu.VMEM((ts, D_ATTN), jnp.bfloat16),
            pltpu.VMEM((ts, D_ATTN), jnp.float32),
            pltpu.VMEM((ts, D_CONV), jnp.bfloat16),
            pltpu.VMEM((ts, D_ATTN), jnp.float32),
        ],
        compiler_params=pltpu.CompilerParams(
            dimension_semantics=("arbitrary", "arbitrary"),
            vmem_limit_bytes=VMEM_LIMIT_BYTES),
        name="hybrid_layer",
    )(x, norm_in, w_in[0].astype(jnp.bfloat16), conv_w[0], attn_sinks, norm_conv_out, norm_attn_out,
      w_out[0].astype(jnp.bfloat16), norm_final[None, :], _alibi_bias())
```

```python
import functools

import jax
import jax.numpy as jnp
from jax import lax
from jax.experimental import pallas as pl
from jax.experimental.pallas import tpu as pltpu

D_MODEL = 1024
D_CONV = 1024
CONV_WIDTH = 3
N_Q_HEADS = 16
N_KV_HEADS = 2
HEAD_DIM = 64
Q_PER_KV = N_Q_HEADS // N_KV_HEADS
D_ATTN = N_Q_HEADS * HEAD_DIM
D_KV = N_KV_HEADS * HEAD_DIM
BLOCK = 128
D_MIX = D_CONV + D_ATTN
D_IN_PROJ = 4 * D_CONV + 2 * D_ATTN + 2 * D_KV
RMS_EPS = 1e-5
LOG2E = 1.4426950408889634

OFF_CB, OFF_CC, OFF_CU, OFF_GC = 0, D_CONV, 2 * D_CONV, 3 * D_CONV
OFF_Q = 4 * D_CONV
OFF_KV = OFF_Q + D_ATTN
OFF_GA = OFF_KV + 2 * D_KV

LANES = 128
SUBLANES = 8
PAIRS_PER_KV = Q_PER_KV * HEAD_DIM // LANES
SEQ_TILE = 512
CONV_CHUNK = 256
VMEM_LIMIT_BYTES = 56 * 1024 * 1024

_NT = (((1,), (1,)), ((), ()))


def _rms(x, gain):
    return x * lax.rsqrt(jnp.mean(x * x, axis=-1, keepdims=True) + RMS_EPS) * gain


def _silu(x):
    return x * (1.0 / (1.0 + jnp.exp2(x * -LOG2E)))


def _after(value, anchor):
    return jnp.where(anchor > jnp.inf, jnp.zeros_like(value), value)


def _layer_kernel(xcur_ref, xnext_ref, gin_ref, win_ref, convw_ref, sinks_ref, gconv_ref, gattn_ref,
                  wout_ref, gfin_ref, bias_ref, o_ref, hbuf, xrbuf, ubuf, kbuf, vtbuf, qbuf, attn_buf, cbuf,
                  sgabuf, *, n_tiles, tiles_per_seq):
    t = pl.program_id(0)

    @pl.when(t == 0)
    def _():
        hbuf[...] = _rms(xcur_ref[...], gin_ref[...]).astype(jnp.bfloat16)
        xrbuf[...] = jnp.zeros(xrbuf.shape, jnp.float32)

    @pl.when(t < n_tiles)
    def _():
        _tile_body(t % tiles_per_seq, xcur_ref, xnext_ref, gin_ref, win_ref, convw_ref, sinks_ref, gconv_ref,
                   gattn_ref, wout_ref, gfin_ref, bias_ref, o_ref, hbuf, xrbuf, ubuf, kbuf, vtbuf, qbuf,
                   attn_buf, cbuf, sgabuf)

    @pl.when(t == n_tiles)
    def _():
        o_ref[...] = _rms(xrbuf[...], gfin_ref[...])


def _tile_body(j, xcur_ref, xnext_ref, gin_ref, win_ref, convw_ref, sinks_ref, gconv_ref, gattn_ref,
               wout_ref, gfin_ref, bias_ref, o_ref, hbuf, xrbuf, ubuf, kbuf, vtbuf, qbuf, attn_buf, cbuf, sgabuf):
    ts = xcur_ref.shape[0]
    f32, bf16 = jnp.float32, jnp.bfloat16

    @pl.when(j == 0)
    def _():
        ubuf[0:SUBLANES, :] = jnp.zeros((SUBLANES, D_CONV), f32)
        kbuf[:, 0:BLOCK, :] = jnp.zeros((4, BLOCK, LANES), bf16)
        vtbuf[:, :, 0:BLOCK] = jnp.zeros((4, LANES, BLOCK), bf16)

    def proj(off, width):
        return jnp.dot(hbuf[...], win_ref[:, off:off + width], preferred_element_type=f32)

    q = proj(OFF_Q, D_ATTN) * (HEAD_DIM ** -0.5 * LOG2E)
    out_prev = _rms(xrbuf[...], gfin_ref[...])
    o_ref[...] = out_prev
    qbuf[...] = _after(q, out_prev).astype(bf16)
    kv = proj(OFF_KV, 2 * D_KV)
    k = kv[:, :D_KV]
    lo = lax.broadcasted_iota(jnp.int32, (ts, LANES), 1) < HEAD_DIM
    k_swapped = pltpu.roll(k, HEAD_DIM, axis=1)
    zero = jnp.zeros_like(k)
    for i, kk in enumerate([jnp.where(lo, k, zero), jnp.where(lo, zero, k_swapped),
                            jnp.where(lo, k_swapped, zero), jnp.where(lo, zero, k)]):
        kbuf[i, BLOCK:BLOCK + ts, :] = kk.astype(bf16)
    vt = kv[:, D_KV:].T.astype(bf16)
    vt0, vt1 = vt[:HEAD_DIM, :], vt[HEAD_DIM:, :]
    zrows = jnp.zeros_like(vt0)
    for i, vv in enumerate([(vt0, zrows), (zrows, vt0), (vt1, zrows), (zrows, vt1)]):
        vtbuf[i, :, BLOCK:BLOCK + ts] = jnp.concatenate(vv, axis=0)

    ki = lax.broadcasted_iota(jnp.int32, (BLOCK, BLOCK), 0)
    qj = lax.broadcasted_iota(jnp.int32, (BLOCK, BLOCK), 1)
    tri = ki <= qj
    neg = jnp.where(j == 0, -jnp.inf, 0.0).astype(f32)

    chunks = [(qb, g) for qb in range(ts // BLOCK) for g in range(N_KV_HEADS)]

    def scores(chunk):
        qb, g = chunk
        r0, t0 = qb * BLOCK, g * PAIRS_PER_KV
        qs = jnp.concatenate(
            [qbuf[r0:r0 + BLOCK, (t0 + jp) * LANES:(t0 + jp + 1) * LANES] for jp in range(PAIRS_PER_KV)],
            axis=0)
        kcat = jnp.concatenate([kbuf[2 * g, r0:r0 + 2 * BLOCK, :],
                                kbuf[2 * g + 1, r0:r0 + 2 * BLOCK, :]], axis=0)
        return lax.dot_general(kcat, qs, _NT, preferred_element_type=f32)

    def softmax(chunk, s_all):
        qb, g = chunk
        p_blocks = [[None] * PAIRS_PER_KV for _ in range(4)]
        for jp in range(PAIRS_PER_KV):
            for par in range(2):
                head = g * Q_PER_KV + 2 * jp + par
                k0 = par * 2 * BLOCK
                cols = slice(jp * BLOCK, (jp + 1) * BLOCK)
                s_prev = s_all[k0:k0 + BLOCK, cols]
                s_cur = s_all[k0 + BLOCK:k0 + 2 * BLOCK, cols]
                if qb == 0:
                    s_prev = s_prev + neg
                s = jnp.where(tri, s_cur, s_prev) + bias_ref[head]
                sink = sinks_ref[0, head] * LOG2E
                m = jnp.maximum(jnp.max(s, axis=0, keepdims=True), sink)
                p = jnp.exp2(s - m)
                denom = jnp.sum(p, axis=0, keepdims=True) + jnp.exp2(sink - m)
                pn = p * (1.0 / denom)
                zero_p = jnp.zeros_like(pn)
                p_blocks[2 * par][jp] = jnp.where(tri, zero_p, pn).astype(bf16)
                p_blocks[2 * par + 1][jp] = jnp.where(tri, pn, zero_p).astype(bf16)
        return jnp.concatenate([jnp.concatenate(row, axis=1) for row in p_blocks], axis=0)

    def weighted_values(chunk, p_all):
        qb, g = chunk
        r0, t0 = qb * BLOCK, g * PAIRS_PER_KV
        vtcat = jnp.concatenate([vtbuf[2 * g, :, r0:r0 + 2 * BLOCK],
                                 vtbuf[2 * g + 1, :, r0:r0 + 2 * BLOCK]], axis=1)
        o_all = jnp.dot(vtcat, p_all, preferred_element_type=f32)
        for jp in range(PAIRS_PER_KV):
            attn_buf[r0:r0 + BLOCK, (t0 + jp) * LANES:(t0 + jp + 1) * LANES] = \
                o_all[:, jp * BLOCK:(jp + 1) * BLOCK].T

    def conv_elementwise(c, cc, cu, cb, gc):
        cs = slice(c * CONV_CHUNK, (c + 1) * CONV_CHUNK)
        u = cc * cu
        ubuf[SUBLANES:SUBLANES + ts, cs] = u
        u1 = ubuf[SUBLANES - 1:SUBLANES - 1 + ts, cs]
        u2 = ubuf[SUBLANES - 2:SUBLANES - 2 + ts, cs]
        ubuf[0:SUBLANES, cs] = ubuf[ts:ts + SUBLANES, cs]
        y = cb * (convw_ref[2:3, cs] * u + convw_ref[1:2, cs] * u1 + convw_ref[0:1, cs] * u2)
        cbuf[:, cs] = (y * gconv_ref[:, cs] * _silu(gc)).astype(bf16)
        return jnp.sum(y * y, axis=-1, keepdims=True)

    n_conv = D_CONV // CONV_CHUNK
    sgabuf[...] = _silu(proj(OFF_GA, D_ATTN))
    s_ahead = [scores(chunks[0]), scores(chunks[1])]
    state = dict(next_chunk=0, ssq_c=jnp.zeros((ts, 1), f32), r_a=None, attn_y=None)
    projected = [[] for _ in range(n_conv)]

    def attention_block():
        i = state["next_chunk"]
        state["next_chunk"] = i + 1
        weighted_values(chunks[i], softmax(chunks[i], s_ahead.pop(0)))
        if i + 2 < len(chunks):
            s_ahead.append(scores(chunks[i + 2]))

    def conv_block(c):
        state["ssq_c"] = state["ssq_c"] + conv_elementwise(c, *projected[c])

    def attention_gate_block():
        attn = attn_buf[...]
        state["r_a"] = lax.rsqrt(jnp.mean(attn * attn, axis=-1, keepdims=True) + RMS_EPS)
        state["attn_y"] = (attn * gattn_ref[...] * sgabuf[...]).astype(bf16)

    A, G = attention_block, attention_gate_block
    after_dot = [[A, A, A, None],
                 [A, A, A, functools.partial(conv_block, 0)],
                 [A, A, functools.partial(conv_block, 1), G],
                 [functools.partial(conv_block, 2), None, None, None]]
    assert len(after_dot) == n_conv and sum(row.count(A) for row in after_dot) == len(chunks)
    for c in range(n_conv):
        for off, block in zip((OFF_CC, OFF_CU, OFF_CB, OFF_GC), after_dot[c]):
            projected[c].append(proj(off + c * CONV_CHUNK, CONV_CHUNK))
            if block is not None:
                block()

    kbuf[:, 0:BLOCK, :] = kbuf[:, ts:ts + BLOCK, :]
    vtbuf[:, :, 0:BLOCK] = vtbuf[:, :, ts:ts + BLOCK]

    o_a = jnp.dot(state["attn_y"], wout_ref[D_CONV:D_MIX, :], preferred_element_type=f32)
    conv_block(n_conv - 1)
    x_next = xnext_ref[...]
    r_next = lax.rsqrt(jnp.mean(x_next * x_next, axis=-1, keepdims=True) + RMS_EPS)
    h_next = (x_next * _after(r_next, o_a[:, 0:1]) * gin_ref[...]).astype(bf16)
    hbuf[...] = h_next
    o_c = jnp.dot(_after(cbuf[...], h_next), wout_ref[0:D_CONV, :], preferred_element_type=f32)
    r_c = lax.rsqrt(state["ssq_c"] * (1.0 / D_CONV) + RMS_EPS)
    xrbuf[...] = xcur_ref[...] + state["r_a"] * o_a + r_c * o_c


def _alibi_bias():
    slopes = jnp.exp2(-8.0 * jnp.arange(1, N_Q_HEADS + 1, dtype=jnp.float32) / N_Q_HEADS)
    ki = jnp.arange(BLOCK)[:, None]
    qj = jnp.arange(BLOCK)[None, :]
    dist = jnp.where(ki <= qj, qj - ki, BLOCK + qj - ki).astype(jnp.float32)
    return -slopes[:, None, None] * dist[None] * LOG2E


@jax.jit
def kernel(x, norm_in, w_in, conv_w, attn_sinks, norm_conv_out, norm_attn_out, w_out, norm_final):
    bsz, seq, d = x.shape
    assert d == D_MODEL and seq % SEQ_TILE == 0 and SEQ_TILE % BLOCK == 0
    assert w_in.shape == (1, D_MODEL, D_IN_PROJ) and w_out.shape == (1, D_MIX, D_MODEL)
    ts = SEQ_TILE
    tiles_per_seq = seq // ts
    n_tiles = bsz * tiles_per_seq

    def x_tile(shift):
        def index_map(t):
            i = jnp.clip(t + shift, 0, n_tiles - 1)
            return (i // tiles_per_seq, i % tiles_per_seq, 0)
        return pl.BlockSpec((None, ts, D_MODEL), index_map)

    const = lambda shape: pl.BlockSpec(shape, lambda t: (0,) * len(shape), pipeline_mode=pl.Buffered(1))
    return pl.pallas_call(
        functools.partial(_layer_kernel, n_tiles=n_tiles, tiles_per_seq=tiles_per_seq),
        grid=(n_tiles + 1,),
        in_specs=[
            x_tile(0),
            x_tile(1),
            const((1, D_MODEL)),
            const((D_MODEL, D_IN_PROJ)),
            const((CONV_WIDTH, D_CONV)),
            pl.BlockSpec(memory_space=pltpu.SMEM),
            const((1, D_CONV)),
            const((1, D_ATTN)),
            const((D_MIX, D_MODEL)),
            const((1, D_MODEL)),
            const((N_Q_HEADS, BLOCK, BLOCK)),
        ],
        out_specs=x_tile(-1),
        out_shape=jax.ShapeDtypeStruct(x.shape, x.dtype),
        scratch_shapes=[
            pltpu.VMEM((ts, D_MODEL), jnp.bfloat16),
            pltpu.VMEM((ts, D_MODEL), jnp.float32),
            pltpu.VMEM((SUBLANES + ts, D_CONV), jnp.float32),
            pltpu.VMEM((4, BLOCK + ts, LANES), jnp.bfloat16),
            pltpu.VMEM((4, LANES, BLOCK + ts), jnp.bfloat16),
            pltpu.VMEM((ts, D_ATTN), jnp.bfloat16),
            pltpu.VMEM((ts, D_ATTN), jnp.float32),
            pltpu.VMEM((ts, D_CONV), jnp.bfloat16),
            pltpu.VMEM((ts, D_ATTN), jnp.float32),
        ],
        compiler_params=pltpu.CompilerParams(
            dimension_semantics=("arbitrary",),
            vmem_limit_bytes=VMEM_LIMIT_BYTES),
        name="hybrid_layer",
    )(x, x, norm_in, w_in[0].astype(jnp.bfloat16), conv_w[0], attn_sinks, norm_conv_out, norm_attn_out,
      w_out[0].astype(jnp.bfloat16), norm_final[None, :], _alibi_bias())
```

```python
import jax
import jax.numpy as jnp
from jax import lax
from jax.experimental import pallas as pl
from jax.experimental.pallas import tpu as pltpu

D_MODEL = 1024
D_CONV = 1024
CONV_WIDTH = 3
N_Q_HEADS = 16
N_KV_HEADS = 2
HEAD_DIM = 64
Q_PER_KV = N_Q_HEADS // N_KV_HEADS
D_ATTN = N_Q_HEADS * HEAD_DIM
D_KV = N_KV_HEADS * HEAD_DIM
BLOCK = 128
D_MIX = D_CONV + D_ATTN
D_IN_PROJ = 4 * D_CONV + 2 * D_ATTN + 2 * D_KV
RMS_EPS = 1e-5
LOG2E = 1.4426950408889634

OFF_CB, OFF_CC, OFF_CU, OFF_GC = 0, D_CONV, 2 * D_CONV, 3 * D_CONV
OFF_Q = 4 * D_CONV
OFF_KV = OFF_Q + D_ATTN
OFF_GA = OFF_KV + 2 * D_KV

LANES = 128
SUBLANES = 8
PAIRS_PER_KV = Q_PER_KV * HEAD_DIM // LANES
SEQ_TILE = 512
CONV_CHUNK = 256
VMEM_LIMIT_BYTES = 60 * 1024 * 1024

_NT = (((1,), (1,)), ((), ()))


def _rms(x, gain):
    return x * lax.rsqrt(jnp.mean(x * x, axis=-1, keepdims=True) + RMS_EPS) * gain


def _silu(x):
    return x * (1.0 / (1.0 + jnp.exp2(x * -LOG2E)))


def _layer_kernel(x_ref, gin_ref, win_ref, convw_ref, sinks_ref, gconv_ref, gattn_ref,
                  wout_ref, gfin_ref, bias_ref, o_ref, ubuf, kbuf, vtbuf, qbuf, attn_buf, cbuf, sgabuf):
    ts = x_ref.shape[0]
    j = pl.program_id(1)
    f32, bf16 = jnp.float32, jnp.bfloat16

    @pl.when(j == 0)
    def _():
        ubuf[0:SUBLANES, :] = jnp.zeros((SUBLANES, D_CONV), f32)
        kbuf[:, 0:BLOCK, :] = jnp.zeros((4, BLOCK, LANES), bf16)
        vtbuf[:, :, 0:BLOCK] = jnp.zeros((4, LANES, BLOCK), bf16)

    x = x_ref[...]
    h = _rms(x, gin_ref[...])

    def proj(off, width):
        return jnp.dot(h, win_ref[:, off:off + width], preferred_element_type=f32)

    qbuf[...] = (proj(OFF_Q, D_ATTN) * (HEAD_DIM ** -0.5 * LOG2E)).astype(bf16)
    kv = proj(OFF_KV, 2 * D_KV)
    k = kv[:, :D_KV]
    lo = lax.broadcasted_iota(jnp.int32, (ts, LANES), 1) < HEAD_DIM
    k_swapped = pltpu.roll(k, HEAD_DIM, axis=1)
    zero = jnp.zeros_like(k)
    for i, kk in enumerate([jnp.where(lo, k, zero), jnp.where(lo, zero, k_swapped),
                            jnp.where(lo, k_swapped, zero), jnp.where(lo, zero, k)]):
        kbuf[i, BLOCK:BLOCK + ts, :] = kk.astype(bf16)
    vt = kv[:, D_KV:].T.astype(bf16)
    vt0, vt1 = vt[:HEAD_DIM, :], vt[HEAD_DIM:, :]
    zrows = jnp.zeros_like(vt0)
    for i, vv in enumerate([(vt0, zrows), (zrows, vt0), (vt1, zrows), (zrows, vt1)]):
        vtbuf[i, :, BLOCK:BLOCK + ts] = jnp.concatenate(vv, axis=0)

    ki = lax.broadcasted_iota(jnp.int32, (BLOCK, BLOCK), 0)
    qj = lax.broadcasted_iota(jnp.int32, (BLOCK, BLOCK), 1)
    tri = ki <= qj
    neg = jnp.where(j == 0, -jnp.inf, 0.0).astype(f32)

    chunks = [(qb, g) for qb in range(ts // BLOCK) for g in range(N_KV_HEADS)]

    def scores(chunk):
        qb, g = chunk
        r0, t0 = qb * BLOCK, g * PAIRS_PER_KV
        qs = jnp.concatenate(
            [qbuf[r0:r0 + BLOCK, (t0 + jp) * LANES:(t0 + jp + 1) * LANES] for jp in range(PAIRS_PER_KV)],
            axis=0)
        kcat = jnp.concatenate([kbuf[2 * g, r0:r0 + 2 * BLOCK, :],
                                kbuf[2 * g + 1, r0:r0 + 2 * BLOCK, :]], axis=0)
        return lax.dot_general(kcat, qs, _NT, preferred_element_type=f32)

    def softmax(chunk, s_all):
        qb, g = chunk
        p_blocks = [[None] * PAIRS_PER_KV for _ in range(4)]
        for jp in range(PAIRS_PER_KV):
            for par in range(2):
                head = g * Q_PER_KV + 2 * jp + par
                k0 = par * 2 * BLOCK
                cols = slice(jp * BLOCK, (jp + 1) * BLOCK)
                s_prev = s_all[k0:k0 + BLOCK, cols]
                s_cur = s_all[k0 + BLOCK:k0 + 2 * BLOCK, cols]
                if qb == 0:
                    s_prev = s_prev + neg
                s = jnp.where(tri, s_cur, s_prev) + bias_ref[head]
                sink = sinks_ref[0, head] * LOG2E
                m = jnp.maximum(jnp.max(s, axis=0, keepdims=True), sink)
                p = jnp.exp2(s - m)
                denom = jnp.sum(p, axis=0, keepdims=True) + jnp.exp2(sink - m)
                pn = p * (1.0 / denom)
                zero_p = jnp.zeros_like(pn)
                p_blocks[2 * par][jp] = jnp.where(tri, zero_p, pn).astype(bf16)
                p_blocks[2 * par + 1][jp] = jnp.where(tri, pn, zero_p).astype(bf16)
        return jnp.concatenate([jnp.concatenate(row, axis=1) for row in p_blocks], axis=0)

    def weighted_values(chunk, p_all):
        qb, g = chunk
        r0, t0 = qb * BLOCK, g * PAIRS_PER_KV
        vtcat = jnp.concatenate([vtbuf[2 * g, :, r0:r0 + 2 * BLOCK],
                                 vtbuf[2 * g + 1, :, r0:r0 + 2 * BLOCK]], axis=1)
        o_all = jnp.dot(vtcat, p_all, preferred_element_type=f32)
        for jp in range(PAIRS_PER_KV):
            attn_buf[r0:r0 + BLOCK, (t0 + jp) * LANES:(t0 + jp + 1) * LANES] = \
                o_all[:, jp * BLOCK:(jp + 1) * BLOCK].T

    def conv_elementwise(c, cc, cu, cb, gc):
        cs = slice(c * CONV_CHUNK, (c + 1) * CONV_CHUNK)
        u = cc * cu
        ubuf[SUBLANES:SUBLANES + ts, cs] = u
        u1 = ubuf[SUBLANES - 1:SUBLANES - 1 + ts, cs]
        u2 = ubuf[SUBLANES - 2:SUBLANES - 2 + ts, cs]
        ubuf[0:SUBLANES, cs] = ubuf[ts:ts + SUBLANES, cs]
        y = cb * (convw_ref[2:3, cs] * u + convw_ref[1:2, cs] * u1 + convw_ref[0:1, cs] * u2)
        cbuf[:, cs] = y * gconv_ref[:, cs] * _silu(gc)
        return jnp.sum(y * y, axis=-1, keepdims=True)

    n_conv = D_CONV // CONV_CHUNK
    assert len(chunks) == 2 * n_conv
    sgabuf[...] = _silu(proj(OFF_GA, D_ATTN))
    s_ahead = [scores(chunks[0]), scores(chunks[1])]
    ssq_c = jnp.zeros((ts, 1), f32)
    previous = None
    for c in range(n_conv):
        projected = []
        for i, off in ((2 * c, OFF_CC), (2 * c + 1, OFF_CU)):
            projected.append(proj(off + c * CONV_CHUNK, CONV_CHUNK))
            weighted_values(chunks[i], softmax(chunks[i], s_ahead.pop(0)))
            if i + 2 < len(chunks):
                s_ahead.append(scores(chunks[i + 2]))
        projected.append(proj(OFF_CB + c * CONV_CHUNK, CONV_CHUNK))
        if previous is not None:
            ssq_c = ssq_c + conv_elementwise(c - 1, *previous)
        projected.append(proj(OFF_GC + c * CONV_CHUNK, CONV_CHUNK))
        previous = projected

    kbuf[:, 0:BLOCK, :] = kbuf[:, ts:ts + BLOCK, :]
    vtbuf[:, :, 0:BLOCK] = vtbuf[:, :, ts:ts + BLOCK]

    attn = attn_buf[...]
    attn_y = attn * gattn_ref[...] * sgabuf[...]
    split = (n_conv - 1) * CONV_CHUNK
    o_c = jnp.dot(cbuf[:, 0:split], wout_ref[0:split, :], preferred_element_type=f32)
    ssq_c = ssq_c + conv_elementwise(n_conv - 1, *previous)
    o_a = jnp.dot(attn_y, wout_ref[D_CONV:D_MIX, :], preferred_element_type=f32)
    o_c = o_c + jnp.dot(cbuf[:, split:D_CONV], wout_ref[split:D_CONV, :], preferred_element_type=f32)
    r_a = lax.rsqrt(jnp.mean(attn * attn, axis=-1, keepdims=True) + RMS_EPS)
    r_c = lax.rsqrt(ssq_c * (1.0 / D_CONV) + RMS_EPS)
    o_ref[...] = _rms(x + r_a * o_a + r_c * o_c, gfin_ref[...])


def _alibi_bias():
    slopes = jnp.exp2(-8.0 * jnp.arange(1, N_Q_HEADS + 1, dtype=jnp.float32) / N_Q_HEADS)
    ki = jnp.arange(BLOCK)[:, None]
    qj = jnp.arange(BLOCK)[None, :]
    dist = jnp.where(ki <= qj, qj - ki, BLOCK + qj - ki).astype(jnp.float32)
    return -slopes[:, None, None] * dist[None] * LOG2E


@jax.jit
def kernel(x, norm_in, w_in, conv_w, attn_sinks, norm_conv_out, norm_attn_out, w_out, norm_final):
    bsz, seq, d = x.shape
    assert d == D_MODEL and seq % SEQ_TILE == 0 and SEQ_TILE % BLOCK == 0
    assert w_in.shape == (1, D_MODEL, D_IN_PROJ) and w_out.shape == (1, D_MIX, D_MODEL)
    ts = SEQ_TILE
    const = lambda shape: pl.BlockSpec(shape, lambda b, j: (0,) * len(shape),
                                       pipeline_mode=pl.Buffered(1))
    return pl.pallas_call(
        _layer_kernel,
        grid=(bsz, seq // ts),
        in_specs=[
            pl.BlockSpec((None, ts, D_MODEL), lambda b, j: (b, j, 0)),
            const((1, D_MODEL)),
            const((D_MODEL, D_IN_PROJ)),
            const((CONV_WIDTH, D_CONV)),
            pl.BlockSpec(memory_space=pltpu.SMEM),
            const((1, D_CONV)),
            const((1, D_ATTN)),
            const((D_MIX, D_MODEL)),
            const((1, D_MODEL)),
            const((N_Q_HEADS, BLOCK, BLOCK)),
        ],
        out_specs=pl.BlockSpec((None, ts, D_MODEL), lambda b, j: (b, j, 0)),
        out_shape=jax.ShapeDtypeStruct(x.shape, x.dtype),
        scratch_shapes=[
            pltpu.VMEM((SUBLANES + ts, D_CONV), jnp.float32),
            pltpu.VMEM((4, BLOCK + ts, LANES), jnp.bfloat16),
            pltpu.VMEM((4, LANES, BLOCK + ts), jnp.bfloat16),
            pltpu.VMEM((ts, D_ATTN), jnp.bfloat16),
            pltpu.VMEM((ts, D_ATTN), jnp.float32),
            pltpu.VMEM((ts, D_CONV), jnp.float32),
            pltpu.VMEM((ts, D_ATTN), jnp.float32),
        ],
        compiler_params=pltpu.CompilerParams(
            dimension_semantics=("arbitrary", "arbitrary"),
            vmem_limit_bytes=VMEM_LIMIT_BYTES),
        name="hybrid_layer",
    )(x, norm_in, w_in[0], conv_w[0], attn_sinks, norm_conv_out, norm_attn_out, w_out[0],
      norm_final[None, :], _alibi_bias())
```

```python
import jax
import jax.numpy as jnp
from jax import lax
from jax.experimental import pallas as pl
from jax.experimental.pallas import tpu as pltpu

D_MODEL = 1024
D_CONV = 1024
CONV_WIDTH = 3
N_Q_HEADS = 16
N_KV_HEADS = 2
HEAD_DIM = 64
Q_PER_KV = N_Q_HEADS // N_KV_HEADS
D_ATTN = N_Q_HEADS * HEAD_DIM
D_KV = N_KV_HEADS * HEAD_DIM
BLOCK = 128
D_MIX = D_CONV + D_ATTN
D_IN_PROJ = 4 * D_CONV + 2 * D_ATTN + 2 * D_KV
RMS_EPS = 1e-5
LOG2E = 1.4426950408889634

OFF_CB, OFF_CC, OFF_CU, OFF_GC = 0, D_CONV, 2 * D_CONV, 3 * D_CONV
OFF_Q = 4 * D_CONV
OFF_KV = OFF_Q + D_ATTN
OFF_GA = OFF_KV + 2 * D_KV

LANES = 128
SUBLANES = 8
PAIRS_PER_KV = Q_PER_KV * HEAD_DIM // LANES
SEQ_TILE = 512
CONV_CHUNK = 256
W_IN_STAGE_ROWS = 64
W_OUT_STAGE_ROWS = 256
VMEM_LIMIT_BYTES = 56 * 1024 * 1024

_NT = (((1,), (1,)), ((), ()))


def _rms(x, gain):
    return x * lax.rsqrt(jnp.mean(x * x, axis=-1, keepdims=True) + RMS_EPS) * gain


def _silu(x):
    return x * (1.0 / (1.0 + jnp.exp2(x * -LOG2E)))


def _load_as_bf16(w_hbm, w_vmem, stage, sems):
    chunk = stage.shape[1]
    n_chunks = w_hbm.shape[0] // chunk
    assert n_chunks * chunk == w_hbm.shape[0] and w_vmem.shape == w_hbm.shape

    def copy(i):
        return pltpu.make_async_copy(w_hbm.at[pl.ds(i * chunk, chunk), :], stage.at[i % 2], sems.at[i % 2])

    copy(0).start()
    for i in range(n_chunks):
        if i + 1 < n_chunks:
            copy(i + 1).start()
        copy(i).wait()
        w_vmem[i * chunk:(i + 1) * chunk, :] = stage[i % 2].astype(w_vmem.dtype)


def _layer_kernel(x_ref, gin_ref, win_hbm, convw_ref, sinks_ref, gconv_ref, gattn_ref,
                  wout_hbm, gfin_ref, bias_ref, o_ref, win_ref, wout_ref, stage_in, stage_out, sems_in,
                  sems_out, ubuf, kbuf, vtbuf, qbuf, attn_buf, cbuf, sgabuf):
    ts = x_ref.shape[0]
    j = pl.program_id(1)
    f32, bf16 = jnp.float32, jnp.bfloat16

    @pl.when((pl.program_id(0) == 0) & (j == 0))
    def _():
        _load_as_bf16(win_hbm, win_ref, stage_in, sems_in)
        _load_as_bf16(wout_hbm, wout_ref, stage_out, sems_out)

    @pl.when(j == 0)
    def _():
        ubuf[0:SUBLANES, :] = jnp.zeros((SUBLANES, D_CONV), f32)
        kbuf[:, 0:BLOCK, :] = jnp.zeros((4, BLOCK, LANES), bf16)
        vtbuf[:, :, 0:BLOCK] = jnp.zeros((4, LANES, BLOCK), bf16)

    x = x_ref[...]
    h = _rms(x, gin_ref[...]).astype(bf16)

    def proj(off, width):
        return jnp.dot(h, win_ref[:, off:off + width], preferred_element_type=f32)

    qbuf[...] = (proj(OFF_Q, D_ATTN) * (HEAD_DIM ** -0.5 * LOG2E)).astype(bf16)
    kv = proj(OFF_KV, 2 * D_KV)
    k = kv[:, :D_KV]
    lo = lax.broadcasted_iota(jnp.int32, (ts, LANES), 1) < HEAD_DIM
    k_swapped = pltpu.roll(k, HEAD_DIM, axis=1)
    zero = jnp.zeros_like(k)
    for i, kk in enumerate([jnp.where(lo, k, zero), jnp.where(lo, zero, k_swapped),
                            jnp.where(lo, k_swapped, zero), jnp.where(lo, zero, k)]):
        kbuf[i, BLOCK:BLOCK + ts, :] = kk.astype(bf16)
    vt = kv[:, D_KV:].T.astype(bf16)
    vt0, vt1 = vt[:HEAD_DIM, :], vt[HEAD_DIM:, :]
    zrows = jnp.zeros_like(vt0)
    for i, vv in enumerate([(vt0, zrows), (zrows, vt0), (vt1, zrows), (zrows, vt1)]):
        vtbuf[i, :, BLOCK:BLOCK + ts] = jnp.concatenate(vv, axis=0)

    ki = lax.broadcasted_iota(jnp.int32, (BLOCK, BLOCK), 0)
    qj = lax.broadcasted_iota(jnp.int32, (BLOCK, BLOCK), 1)
    tri = ki <= qj
    neg = jnp.where(j == 0, -jnp.inf, 0.0).astype(f32)

    chunks = [(qb, g) for qb in range(ts // BLOCK) for g in range(N_KV_HEADS)]

    def scores(chunk):
        qb, g = chunk
        r0, t0 = qb * BLOCK, g * PAIRS_PER_KV
        qs = jnp.concatenate(
            [qbuf[r0:r0 + BLOCK, (t0 + jp) * LANES:(t0 + jp + 1) * LANES] for jp in range(PAIRS_PER_KV)],
            axis=0)
        kcat = jnp.concatenate([kbuf[2 * g, r0:r0 + 2 * BLOCK, :],
                                kbuf[2 * g + 1, r0:r0 + 2 * BLOCK, :]], axis=0)
        return lax.dot_general(kcat, qs, _NT, preferred_element_type=f32)

    def softmax(chunk, s_all):
        qb, g = chunk
        p_blocks = [[None] * PAIRS_PER_KV for _ in range(4)]
        inv_sums = [[None] * PAIRS_PER_KV for _ in range(2)]
        for jp in range(PAIRS_PER_KV):
            for par in range(2):
                head = g * Q_PER_KV + 2 * jp + par
                k0 = par * 2 * BLOCK
                cols = slice(jp * BLOCK, (jp + 1) * BLOCK)
                s_prev = s_all[k0:k0 + BLOCK, cols]
                s_cur = s_all[k0 + BLOCK:k0 + 2 * BLOCK, cols]
                if qb == 0:
                    s_prev = s_prev + neg
                s = jnp.where(tri, s_cur, s_prev) + bias_ref[head]
                sink = sinks_ref[0, head] * LOG2E
                m = jnp.maximum(jnp.max(s, axis=0, keepdims=True), sink)
                p = jnp.exp2(s - m)
                inv_sums[par][jp] = 1.0 / (jnp.sum(p, axis=0, keepdims=True) + jnp.exp2(sink - m))
                zero_p = jnp.zeros_like(p)
                p_blocks[2 * par][jp] = jnp.where(tri, zero_p, p).astype(bf16)
                p_blocks[2 * par + 1][jp] = jnp.where(tri, p, zero_p).astype(bf16)
        p_all = jnp.concatenate([jnp.concatenate(row, axis=1) for row in p_blocks], axis=0)
        return p_all, [jnp.concatenate(row, axis=1) for row in inv_sums]

    def weighted_values(chunk, p_all, inv_sums):
        qb, g = chunk
        r0, t0 = qb * BLOCK, g * PAIRS_PER_KV
        vtcat = jnp.concatenate([vtbuf[2 * g, :, r0:r0 + 2 * BLOCK],
                                 vtbuf[2 * g + 1, :, r0:r0 + 2 * BLOCK]], axis=1)
        o_all = jnp.dot(vtcat, p_all, preferred_element_type=f32)
        scale = jnp.concatenate([jnp.broadcast_to(r, (HEAD_DIM, r.shape[1])) for r in inv_sums], axis=0)
        o_all = o_all * scale
        for jp in range(PAIRS_PER_KV):
            attn_buf[r0:r0 + BLOCK, (t0 + jp) * LANES:(t0 + jp + 1) * LANES] = \
                o_all[:, jp * BLOCK:(jp + 1) * BLOCK].T

    def conv_elementwise(c, cc, cu, cb, gc):
        cs = slice(c * CONV_CHUNK, (c + 1) * CONV_CHUNK)
        u = cc * cu
        ubuf[SUBLANES:SUBLANES + ts, cs] = u
        u1 = ubuf[SUBLANES - 1:SUBLANES - 1 + ts, cs]
        u2 = ubuf[SUBLANES - 2:SUBLANES - 2 + ts, cs]
        ubuf[0:SUBLANES, cs] = ubuf[ts:ts + SUBLANES, cs]
        y = cb * (convw_ref[2:3, cs] * u + convw_ref[1:2, cs] * u1 + convw_ref[0:1, cs] * u2)
        cbuf[:, cs] = (y * gconv_ref[:, cs] * _silu(gc)).astype(bf16)
        return jnp.sum(y * y, axis=-1, keepdims=True)

    n_conv = D_CONV // CONV_CHUNK
    assert len(chunks) == 2 * n_conv
    sgabuf[...] = _silu(proj(OFF_GA, D_ATTN))
    s_ahead = [scores(chunks[0]), scores(chunks[1])]
    ssq_c = jnp.zeros((ts, 1), f32)
    previous = None
    for c in range(n_conv):
        projected = []
        for i, off in ((2 * c, OFF_CC), (2 * c + 1, OFF_CU)):
            projected.append(proj(off + c * CONV_CHUNK, CONV_CHUNK))
            weighted_values(chunks[i], *softmax(chunks[i], s_ahead.pop(0)))
            if i + 2 < len(chunks):
                s_ahead.append(scores(chunks[i + 2]))
        projected.append(proj(OFF_CB + c * CONV_CHUNK, CONV_CHUNK))
        if previous is not None:
            ssq_c = ssq_c + conv_elementwise(c - 1, *previous)
        projected.append(proj(OFF_GC + c * CONV_CHUNK, CONV_CHUNK))
        previous = projected

    kbuf[:, 0:BLOCK, :] = kbuf[:, ts:ts + BLOCK, :]
    vtbuf[:, :, 0:BLOCK] = vtbuf[:, :, ts:ts + BLOCK]

    attn = attn_buf[...]
    attn_y = (attn * gattn_ref[...] * sgabuf[...]).astype(bf16)
    split = (n_conv - 1) * CONV_CHUNK
    o_c = jnp.dot(cbuf[:, 0:split], wout_ref[0:split, :], preferred_element_type=f32)
    ssq_c = ssq_c + conv_elementwise(n_conv - 1, *previous)
    o_a = jnp.dot(attn_y, wout_ref[D_CONV:D_MIX, :], preferred_element_type=f32)
    o_c = o_c + jnp.dot(cbuf[:, split:D_CONV], wout_ref[split:D_CONV, :], preferred_element_type=f32)
    r_a = lax.rsqrt(jnp.mean(attn * attn, axis=-1, keepdims=True) + RMS_EPS)
    r_c = lax.rsqrt(ssq_c * (1.0 / D_CONV) + RMS_EPS)
    o_ref[...] = _rms(x + r_a * o_a + r_c * o_c, gfin_ref[...])


def _alibi_bias():
    slopes = jnp.exp2(-8.0 * jnp.arange(1, N_Q_HEADS + 1, dtype=jnp.float32) / N_Q_HEADS)
    ki = jnp.arange(BLOCK)[:, None]
    qj = jnp.arange(BLOCK)[None, :]
    dist = jnp.where(ki <= qj, qj - ki, BLOCK + qj - ki).astype(jnp.float32)
    return -slopes[:, None, None] * dist[None] * LOG2E


@jax.jit
def kernel(x, norm_in, w_in, conv_w, attn_sinks, norm_conv_out, norm_attn_out, w_out, norm_final):
    bsz, seq, d = x.shape
    assert d == D_MODEL and seq % SEQ_TILE == 0 and SEQ_TILE % BLOCK == 0
    assert w_in.shape == (1, D_MODEL, D_IN_PROJ) and w_out.shape == (1, D_MIX, D_MODEL)
    ts = SEQ_TILE
    const = lambda shape: pl.BlockSpec(shape, lambda b, j: (0,) * len(shape),
                                       pipeline_mode=pl.Buffered(1))
    return pl.pallas_call(
        _layer_kernel,
        grid=(bsz, seq // ts),
        in_specs=[
            pl.BlockSpec((None, ts, D_MODEL), lambda b, j: (b, j, 0)),
            const((1, D_MODEL)),
            pl.BlockSpec(memory_space=pltpu.HBM),
            const((CONV_WIDTH, D_CONV)),
            pl.BlockSpec(memory_space=pltpu.SMEM),
            const((1, D_CONV)),
            const((1, D_ATTN)),
            pl.BlockSpec(memory_space=pltpu.HBM),
            const((1, D_MODEL)),
            const((N_Q_HEADS, BLOCK, BLOCK)),
        ],
        out_specs=pl.BlockSpec((None, ts, D_MODEL), lambda b, j: (b, j, 0)),
        out_shape=jax.ShapeDtypeStruct(x.shape, x.dtype),
        scratch_shapes=[
            pltpu.VMEM((D_MODEL, D_IN_PROJ), jnp.bfloat16),
            pltpu.VMEM((D_MIX, D_MODEL), jnp.bfloat16),
            pltpu.VMEM((2, W_IN_STAGE_ROWS, D_IN_PROJ), jnp.float32),
            pltpu.VMEM((2, W_OUT_STAGE_ROWS, D_MODEL), jnp.float32),
            pltpu.SemaphoreType.DMA((2,)),
            pltpu.SemaphoreType.DMA((2,)),
            pltpu.VMEM((SUBLANES + ts, D_CONV), jnp.float32),
            pltpu.VMEM((4, BLOCK + ts, LANES), jnp.bfloat16),
            pltpu.VMEM((4, LANES, BLOCK + ts), jnp.bfloat16),
            pltpu.VMEM((ts, D_ATTN), jnp.bfloat16),
            pltpu.VMEM((ts, D_ATTN), jnp.float32),
            pltpu.VMEM((ts, D_CONV), jnp.bfloat16),
            pltpu.VMEM((ts, D_ATTN), jnp.float32),
        ],
        compiler_params=pltpu.CompilerParams(
            dimension_semantics=("arbitrary", "arbitrary"),
            vmem_limit_bytes=VMEM_LIMIT_BYTES),
        name="hybrid_layer",
    )(x, norm_in, w_in[0], conv_w[0], attn_sinks, norm_conv_out, norm_attn_out, w_out[0],
      norm_final[None, :], _alibi_bias())
```

```python
import jax
import jax.numpy as jnp
from jax import lax
from jax.experimental import pallas as pl
from jax.experimental.pallas import tpu as pltpu

D_MODEL = 1024
D_CONV = 1024
CONV_WIDTH = 3
N_Q_HEADS = 16
N_KV_HEADS = 2
HEAD_DIM = 64
Q_PER_KV = N_Q_HEADS // N_KV_HEADS
D_ATTN = N_Q_HEADS * HEAD_DIM
D_KV = N_KV_HEADS * HEAD_DIM
BLOCK = 128
D_MIX = D_CONV + D_ATTN
D_IN_PROJ = 4 * D_CONV + 2 * D_ATTN + 2 * D_KV
RMS_EPS = 1e-5
LOG2E = 1.4426950408889634

OFF_CB, OFF_CC, OFF_CU, OFF_GC = 0, D_CONV, 2 * D_CONV, 3 * D_CONV
OFF_Q = 4 * D_CONV
OFF_KV = OFF_Q + D_ATTN
OFF_GA = OFF_KV + 2 * D_KV

LANES = 128
SUBLANES = 8
PAIRS_PER_KV = Q_PER_KV * HEAD_DIM // LANES
SEQ_TILE = 512
CONV_CHUNK = 256
VMEM_LIMIT_BYTES = 60 * 1024 * 1024

_NT = (((1,), (1,)), ((), ()))


def _rms(x, gain):
    return x * lax.rsqrt(jnp.mean(x * x, axis=-1, keepdims=True) + RMS_EPS) * gain


def _silu(x):
    return x * (1.0 / (1.0 + jnp.exp2(x * -LOG2E)))


def _layer_kernel(x_ref, gin_ref, win_ref, convw_ref, gconv_ref, gattn_ref,
                  wout_ref, gfin_ref, bias_ref, sinkv_ref, o_ref, ubuf, kbuf, vtbuf, qbuf, attn_buf, cbuf, sgabuf):
    ts = x_ref.shape[0]
    j = pl.program_id(1)
    f32, bf16 = jnp.float32, jnp.bfloat16

    @pl.when(j == 0)
    def _():
        ubuf[0:SUBLANES, :] = jnp.zeros((SUBLANES, D_CONV), f32)
        kbuf[:, 0:BLOCK, :] = jnp.zeros((4, BLOCK, LANES), bf16)
        vtbuf[:, :, 0:BLOCK] = jnp.zeros((4, LANES, BLOCK), bf16)

    x = x_ref[...]
    h = _rms(x, gin_ref[...])

    def proj(off, width):
        return jnp.dot(h, win_ref[:, off:off + width], preferred_element_type=f32)

    qbuf[...] = (proj(OFF_Q, D_ATTN) * (HEAD_DIM ** -0.5 * LOG2E)).astype(bf16)
    kv = proj(OFF_KV, 2 * D_KV)
    k = kv[:, :D_KV]
    lo = lax.broadcasted_iota(jnp.int32, (ts, LANES), 1) < HEAD_DIM
    k_swapped = pltpu.roll(k, HEAD_DIM, axis=1)
    zero = jnp.zeros_like(k)
    for i, kk in enumerate([jnp.where(lo, k, zero), jnp.where(lo, zero, k_swapped),
                            jnp.where(lo, k_swapped, zero), jnp.where(lo, zero, k)]):
        kbuf[i, BLOCK:BLOCK + ts, :] = kk.astype(bf16)
    vt = kv[:, D_KV:].T.astype(bf16)
    vt0, vt1 = vt[:HEAD_DIM, :], vt[HEAD_DIM:, :]
    zrows = jnp.zeros_like(vt0)
    for i, vv in enumerate([(vt0, zrows), (zrows, vt0), (vt1, zrows), (zrows, vt1)]):
        vtbuf[i, :, BLOCK:BLOCK + ts] = jnp.concatenate(vv, axis=0)

    HALF = BLOCK // 2
    half_lanes = PAIRS_PER_KV * HALF
    kk = lax.broadcasted_iota(jnp.int32, (HALF, half_lanes), 0)
    qq = lax.broadcasted_iota(jnp.int32, (HALF, half_lanes), 1) & (HALF - 1)
    tri = kk <= qq
    neg = jnp.where(j == 0, -jnp.inf, 0.0).astype(f32)

    chunks = [(qb, g) for qb in range(ts // BLOCK) for g in range(N_KV_HEADS)]

    def scores(chunk):
        qb, g = chunk
        r0, t0 = qb * BLOCK, g * PAIRS_PER_KV
        out = []
        for half in range(2):
            q0 = r0 + half * HALF
            qs = jnp.concatenate(
                [qbuf[q0:q0 + HALF, (t0 + jp) * LANES:(t0 + jp + 1) * LANES] for jp in range(PAIRS_PER_KV)],
                axis=0)
            k0 = r0 + half * HALF
            kcat = jnp.concatenate([kbuf[2 * g, k0:k0 + 3 * HALF, :],
                                    kbuf[2 * g + 1, k0:k0 + 3 * HALF, :]], axis=0)
            out.append(lax.dot_general(kcat, qs, _NT, preferred_element_type=f32))
        return out

    def softmax(chunk, s_halves):
        qb, g = chunk
        p_cols, inv_sums = [], [[None, None], [None, None]]
        for half, s_all in enumerate(s_halves):
            p_rows = []
            for par in range(2):
                base = par * 3 * HALF
                if half == 0:
                    prev_lo, prev_hi = s_all[base:base + HALF], s_all[base + HALF:base + 2 * HALF]
                    cur_lo = s_all[base + 2 * HALF:base + 3 * HALF]
                    if qb == 0:
                        prev_lo, prev_hi = prev_lo + neg, prev_hi + neg
                    s = jnp.concatenate([jnp.where(tri, cur_lo, prev_lo), prev_hi], axis=0)
                else:
                    prev_hi = s_all[base:base + HALF]
                    cur_lo, cur_hi = s_all[base + HALF:base + 2 * HALF], s_all[base + 2 * HALF:base + 3 * HALF]
                    if qb == 0:
                        prev_hi = prev_hi + neg
                    s = jnp.concatenate([cur_lo, jnp.where(tri, cur_hi, prev_hi)], axis=0)
                s = s + bias_ref[g, par, half]
                sink = sinkv_ref[g, par]
                m = jnp.maximum(jnp.max(s, axis=0, keepdims=True), sink)
                p = jnp.exp2(s - m)
                inv_sums[par][half] = 1.0 / (jnp.sum(p, axis=0, keepdims=True) + jnp.exp2(sink - m))
                p_lo, p_hi = p[:HALF], p[HALF:]
                zero_p = jnp.zeros_like(p_lo)
                if half == 0:
                    p_prev = [jnp.where(tri, zero_p, p_lo), p_hi]
                    p_cur = [jnp.where(tri, p_lo, zero_p), zero_p]
                else:
                    p_prev = [zero_p, jnp.where(tri, zero_p, p_hi)]
                    p_cur = [p_lo, jnp.where(tri, p_hi, zero_p)]
                p_rows += p_prev + p_cur
            p_cols.append(jnp.concatenate(p_rows, axis=0).astype(bf16))
        inv = [jnp.concatenate(inv_sums[par], axis=1) for par in range(2)]
        return jnp.concatenate(p_cols, axis=1), inv

    def weighted_values(chunk, p_all, inv_sums):
        qb, g = chunk
        r0, t0 = qb * BLOCK, g * PAIRS_PER_KV
        vtcat = jnp.concatenate([vtbuf[2 * g, :, r0:r0 + 2 * BLOCK],
                                 vtbuf[2 * g + 1, :, r0:r0 + 2 * BLOCK]], axis=1)
        o_all = jnp.dot(vtcat, p_all, preferred_element_type=f32)
        scale = jnp.concatenate([jnp.broadcast_to(r, (HEAD_DIM, r.shape[1])) for r in inv_sums], axis=0)
        o_all = o_all * scale
        for half in range(2):
            o_half = o_all[:, half * half_lanes:(half + 1) * half_lanes].T
            for jp in range(PAIRS_PER_KV):
                attn_buf[r0 + half * HALF:r0 + (half + 1) * HALF, (t0 + jp) * LANES:(t0 + jp + 1) * LANES] = \
                    o_half[jp * HALF:(jp + 1) * HALF, :]

    def conv_elementwise(c, cc, cu, cb, gc):
        cs = slice(c * CONV_CHUNK, (c + 1) * CONV_CHUNK)
        u = cc * cu
        ubuf[SUBLANES:SUBLANES + ts, cs] = u
        u1 = ubuf[SUBLANES - 1:SUBLANES - 1 + ts, cs]
        u2 = ubuf[SUBLANES - 2:SUBLANES - 2 + ts, cs]
        ubuf[0:SUBLANES, cs] = ubuf[ts:ts + SUBLANES, cs]
        y = cb * (convw_ref[2:3, cs] * u + convw_ref[1:2, cs] * u1 + convw_ref[0:1, cs] * u2)
        cbuf[:, cs] = y * gconv_ref[:, cs] * _silu(gc)
        return jnp.sum(y * y, axis=-1, keepdims=True)

    n_conv = D_CONV // CONV_CHUNK
    assert len(chunks) == 2 * n_conv
    sgabuf[...] = _silu(proj(OFF_GA, D_ATTN))
    s_ahead = [scores(chunks[0]), scores(chunks[1])]
    ssq_c = jnp.zeros((ts, 1), f32)
    previous = None
    for c in range(n_conv):
        projected = []
        for i, off in ((2 * c, OFF_CC), (2 * c + 1, OFF_CU)):
            projected.append(proj(off + c * CONV_CHUNK, CONV_CHUNK))
            weighted_values(chunks[i], *softmax(chunks[i], s_ahead.pop(0)))
            if i + 2 < len(chunks):
                s_ahead.append(scores(chunks[i + 2]))
        projected.append(proj(OFF_CB + c * CONV_CHUNK, CONV_CHUNK))
        if previous is not None:
            ssq_c = ssq_c + conv_elementwise(c - 1, *previous)
        projected.append(proj(OFF_GC + c * CONV_CHUNK, CONV_CHUNK))
        previous = projected

    kbuf[:, 0:BLOCK, :] = kbuf[:, ts:ts + BLOCK, :]
    vtbuf[:, :, 0:BLOCK] = vtbuf[:, :, ts:ts + BLOCK]

    attn = attn_buf[...]
    attn_y = attn * gattn_ref[...] * sgabuf[...]
    split = (n_conv - 1) * CONV_CHUNK
    o_c = jnp.dot(cbuf[:, 0:split], wout_ref[0:split, :], preferred_element_type=f32)
    ssq_c = ssq_c + conv_elementwise(n_conv - 1, *previous)
    o_a = jnp.dot(attn_y, wout_ref[D_CONV:D_MIX, :], preferred_element_type=f32)
    o_c = o_c + jnp.dot(cbuf[:, split:D_CONV], wout_ref[split:D_CONV, :], preferred_element_type=f32)
    r_a = lax.rsqrt(jnp.mean(attn * attn, axis=-1, keepdims=True) + RMS_EPS)
    r_c = lax.rsqrt(ssq_c * (1.0 / D_CONV) + RMS_EPS)
    o_ref[...] = _rms(x + r_a * o_a + r_c * o_c, gfin_ref[...])


def _attention_tables(attn_sinks):
    half = BLOCK // 2
    slopes = jnp.exp2(-8.0 * jnp.arange(1, N_Q_HEADS + 1, dtype=jnp.float32) / N_Q_HEADS)
    key = jnp.arange(BLOCK)[:, None]
    query = jnp.arange(BLOCK)[None, :]
    dist = jnp.where(key <= query, query - key, BLOCK + query - key).astype(jnp.float32)
    bias = -slopes[:, None, None] * dist[None] * LOG2E
    bias = bias.reshape(N_KV_HEADS, PAIRS_PER_KV, 2, BLOCK, 2, half)
    bias = bias.transpose(0, 2, 4, 3, 1, 5).reshape(N_KV_HEADS, 2, 2, BLOCK, PAIRS_PER_KV * half)
    sinks = (attn_sinks.astype(jnp.float32) * LOG2E).reshape(N_KV_HEADS, PAIRS_PER_KV, 2)
    sinks = jnp.broadcast_to(sinks.transpose(0, 2, 1)[:, :, None, :, None],
                             (N_KV_HEADS, 2, 1, PAIRS_PER_KV, half))
    return bias, sinks.reshape(N_KV_HEADS, 2, 1, PAIRS_PER_KV * half)


@jax.jit
def kernel(x, norm_in, w_in, conv_w, attn_sinks, norm_conv_out, norm_attn_out, w_out, norm_final):
    bsz, seq, d = x.shape
    assert d == D_MODEL and seq % SEQ_TILE == 0 and SEQ_TILE % BLOCK == 0
    assert w_in.shape == (1, D_MODEL, D_IN_PROJ) and w_out.shape == (1, D_MIX, D_MODEL)
    ts = SEQ_TILE
    const = lambda shape: pl.BlockSpec(shape, lambda b, j: (0,) * len(shape),
                                       pipeline_mode=pl.Buffered(1))
    return pl.pallas_call(
        _layer_kernel,
        grid=(bsz, seq // ts),
        in_specs=[
            pl.BlockSpec((None, ts, D_MODEL), lambda b, j: (b, j, 0)),
            const((1, D_MODEL)),
            const((D_MODEL, D_IN_PROJ)),
            const((CONV_WIDTH, D_CONV)),
            const((1, D_CONV)),
            const((1, D_ATTN)),
            const((D_MIX, D_MODEL)),
            const((1, D_MODEL)),
            const((N_KV_HEADS, 2, 2, BLOCK, PAIRS_PER_KV * BLOCK // 2)),
            const((N_KV_HEADS, 2, 1, PAIRS_PER_KV * BLOCK // 2)),
        ],
        out_specs=pl.BlockSpec((None, ts, D_MODEL), lambda b, j: (b, j, 0)),
        out_shape=jax.ShapeDtypeStruct(x.shape, x.dtype),
        scratch_shapes=[
            pltpu.VMEM((SUBLANES + ts, D_CONV), jnp.float32),
            pltpu.VMEM((4, BLOCK + ts, LANES), jnp.bfloat16),
            pltpu.VMEM((4, LANES, BLOCK + ts), jnp.bfloat16),
            pltpu.VMEM((ts, D_ATTN), jnp.bfloat16),
            pltpu.VMEM((ts, D_ATTN), jnp.float32),
            pltpu.VMEM((ts, D_CONV), jnp.float32),
            pltpu.VMEM((ts, D_ATTN), jnp.float32),
        ],
        compiler_params=pltpu.CompilerParams(
            dimension_semantics=("arbitrary", "arbitrary"),
            vmem_limit_bytes=VMEM_LIMIT_BYTES),
        name="hybrid_layer",
    )(x, norm_in, w_in[0], conv_w[0], norm_conv_out, norm_attn_out, w_out[0], norm_final[None, :],
      *_attention_tables(attn_sinks[0]))
```

```python
import jax
import jax.numpy as jnp
import numpy as np
from jax import lax
from jax.experimental import pallas as pl
from jax.experimental.pallas import tpu as pltpu

D_MODEL = 1024
D_CONV = 1024
CONV_WIDTH = 3
N_Q_HEADS = 16
N_KV_HEADS = 2
HEAD_DIM = 64
Q_PER_KV = N_Q_HEADS // N_KV_HEADS
D_ATTN = N_Q_HEADS * HEAD_DIM
D_KV = N_KV_HEADS * HEAD_DIM
BLOCK = 128
D_MIX = D_CONV + D_ATTN
D_IN_PROJ = 4 * D_CONV + 2 * D_ATTN + 2 * D_KV
RMS_EPS = 1e-5
LOG2E = 1.4426950408889634

OFF_CB, OFF_CC, OFF_CU, OFF_GC = 0, D_CONV, 2 * D_CONV, 3 * D_CONV
OFF_Q = 4 * D_CONV
OFF_KV = OFF_Q + D_ATTN
OFF_GA = OFF_KV + 2 * D_KV

LANES = 128
SUBLANES = 8
PAIRS_PER_KV = Q_PER_KV * HEAD_DIM // LANES
SEQ_TILE = 512
CONV_CHUNK = 256
VMEM_LIMIT_BYTES = 60 * 1024 * 1024

_NT = (((1,), (1,)), ((), ()))


def _rms(x, gain):
    return x * lax.rsqrt(jnp.mean(x * x, axis=-1, keepdims=True) + RMS_EPS) * gain


def _silu(x):
    return x * (1.0 / (1.0 + jnp.exp2(x * -LOG2E)))


def _layer_kernel(x_ref, gin_ref, win_ref, convw_ref, sinks_ref, gconv_ref, gattn_ref,
                  wout_ref, gfin_ref, bias_ref, o_ref, ubuf, kbuf, vtbuf, qbuf, attn_buf, cbuf, sgabuf):
    ts = x_ref.shape[0]
    j = pl.program_id(1)
    f32, bf16 = jnp.float32, jnp.bfloat16

    @pl.when(j == 0)
    def _():
        ubuf[0:SUBLANES, :] = jnp.zeros((SUBLANES, D_CONV), f32)
        kbuf[:, 0:BLOCK, :] = jnp.zeros((4, BLOCK, LANES), bf16)
        vtbuf[:, :, 0:BLOCK] = jnp.zeros((4, LANES, BLOCK), bf16)

    x = x_ref[...]
    h = _rms(x, gin_ref[...])

    def proj(off, width):
        return jnp.dot(h, win_ref[:, off:off + width], preferred_element_type=f32)

    qbuf[...] = (proj(OFF_Q, D_ATTN) * (HEAD_DIM ** -0.5 * LOG2E)).astype(bf16)
    kv = proj(OFF_KV, 2 * D_KV)
    k = kv[:, :D_KV]
    lo = lax.broadcasted_iota(jnp.int32, (ts, LANES), 1) < HEAD_DIM
    k_swapped = pltpu.roll(k, HEAD_DIM, axis=1)
    zero = jnp.zeros_like(k)
    for i, kk in enumerate([jnp.where(lo, k, zero), jnp.where(lo, zero, k_swapped),
                            jnp.where(lo, k_swapped, zero), jnp.where(lo, zero, k)]):
        kbuf[i, BLOCK:BLOCK + ts, :] = kk.astype(bf16)
    vt = kv[:, D_KV:].T.astype(bf16)
    vt0, vt1 = vt[:HEAD_DIM, :], vt[HEAD_DIM:, :]
    zrows = jnp.zeros_like(vt0)
    for i, vv in enumerate([(vt0, zrows), (zrows, vt0), (vt1, zrows), (zrows, vt1)]):
        vtbuf[i, :, BLOCK:BLOCK + ts] = jnp.concatenate(vv, axis=0)

    HALF = BLOCK // 2
    half_lanes = PAIRS_PER_KV * HALF
    kk = lax.broadcasted_iota(jnp.int32, (HALF, half_lanes), 0)
    qq = lax.broadcasted_iota(jnp.int32, (HALF, half_lanes), 1) & (HALF - 1)
    tri = kk <= qq
    lane_in_half = lax.broadcasted_iota(jnp.int32, (1, half_lanes), 1)
    neg = jnp.where(j == 0, -jnp.inf, 0.0).astype(f32)

    chunks = [(qb, g) for qb in range(ts // BLOCK) for g in range(N_KV_HEADS)]

    def scores(chunk):
        qb, g = chunk
        r0, t0 = qb * BLOCK, g * PAIRS_PER_KV
        out = []
        for half in range(2):
            q0 = r0 + half * HALF
            qs = jnp.concatenate(
                [qbuf[q0:q0 + HALF, (t0 + jp) * LANES:(t0 + jp + 1) * LANES] for jp in range(PAIRS_PER_KV)],
                axis=0)
            k0 = r0 + half * HALF
            kcat = jnp.concatenate([kbuf[2 * g, k0:k0 + 3 * HALF, :],
                                    kbuf[2 * g + 1, k0:k0 + 3 * HALF, :]], axis=0)
            out.append(lax.dot_general(kcat, qs, _NT, preferred_element_type=f32))
        return out

    def softmax(chunk, s_halves):
        qb, g = chunk
        p_cols, inv_sums = [], [[None, None], [None, None]]
        for half, s_all in enumerate(s_halves):
            p_rows = []
            for par in range(2):
                base = par * 3 * HALF
                if half == 0:
                    prev_lo, prev_hi = s_all[base:base + HALF], s_all[base + HALF:base + 2 * HALF]
                    cur_lo = s_all[base + 2 * HALF:base + 3 * HALF]
                    if qb == 0:
                        prev_lo, prev_hi = prev_lo + neg, prev_hi + neg
                    s = jnp.concatenate([jnp.where(tri, cur_lo, prev_lo), prev_hi], axis=0)
                else:
                    prev_hi = s_all[base:base + HALF]
                    cur_lo, cur_hi = s_all[base + HALF:base + 2 * HALF], s_all[base + 2 * HALF:base + 3 * HALF]
                    if qb == 0:
                        prev_hi = prev_hi + neg
                    s = jnp.concatenate([cur_lo, jnp.where(tri, cur_hi, prev_hi)], axis=0)
                s = s + bias_ref[g, par, half]
                sink = jnp.full((1, half_lanes), sinks_ref[0, g * Q_PER_KV + par] * LOG2E, f32)
                for jp in range(1, PAIRS_PER_KV):
                    sink = jnp.where(lane_in_half >= jp * HALF, sinks_ref[0, g * Q_PER_KV + 2 * jp + par] * LOG2E, sink)
                m = jnp.maximum(jnp.max(s, axis=0, keepdims=True), sink)
                p = jnp.exp2(s - m)
                inv_sums[par][half] = 1.0 / (jnp.sum(p, axis=0, keepdims=True) + jnp.exp2(sink - m))
                p_lo, p_hi = p[:HALF], p[HALF:]
                zero_p = jnp.zeros_like(p_lo)
                if half == 0:
                    p_prev = [jnp.where(tri, zero_p, p_lo), p_hi]
                    p_cur = [jnp.where(tri, p_lo, zero_p), zero_p]
                else:
                    p_prev = [zero_p, jnp.where(tri, zero_p, p_hi)]
                    p_cur = [p_lo, jnp.where(tri, p_hi, zero_p)]
                p_rows += p_prev + p_cur
            p_cols.append(jnp.concatenate(p_rows, axis=0).astype(bf16))
        inv = [jnp.concatenate(inv_sums[par], axis=1) for par in range(2)]
        return jnp.concatenate(p_cols, axis=1), inv

    def weighted_values(chunk, p_all, inv_sums):
        qb, g = chunk
        r0, t0 = qb * BLOCK, g * PAIRS_PER_KV
        vtcat = jnp.concatenate([vtbuf[2 * g, :, r0:r0 + 2 * BLOCK],
                                 vtbuf[2 * g + 1, :, r0:r0 + 2 * BLOCK]], axis=1)
        o_all = jnp.dot(vtcat, p_all, preferred_element_type=f32)
        scale = jnp.concatenate([jnp.broadcast_to(r, (HEAD_DIM, r.shape[1])) for r in inv_sums], axis=0)
        o_all = o_all * scale
        for half in range(2):
            o_half = o_all[:, half * half_lanes:(half + 1) * half_lanes].T
            for jp in range(PAIRS_PER_KV):
                attn_buf[r0 + half * HALF:r0 + (half + 1) * HALF, (t0 + jp) * LANES:(t0 + jp + 1) * LANES] = \
                    o_half[jp * HALF:(jp + 1) * HALF, :]

    def conv_elementwise(c, cc, cu, cb, gc):
        cs = slice(c * CONV_CHUNK, (c + 1) * CONV_CHUNK)
        u = cc * cu
        ubuf[SUBLANES:SUBLANES + ts, cs] = u
        u1 = ubuf[SUBLANES - 1:SUBLANES - 1 + ts, cs]
        u2 = ubuf[SUBLANES - 2:SUBLANES - 2 + ts, cs]
        ubuf[0:SUBLANES, cs] = ubuf[ts:ts + SUBLANES, cs]
        y = cb * (convw_ref[2, :, cs] * u + convw_ref[1, :, cs] * u1 + convw_ref[0, :, cs] * u2)
        cbuf[:, cs] = y * gconv_ref[:, cs] * _silu(gc)
        return jnp.sum(y * y, axis=-1, keepdims=True)

    n_conv = D_CONV // CONV_CHUNK
    assert len(chunks) == 2 * n_conv
    sgabuf[...] = _silu(proj(OFF_GA, D_ATTN))
    s_ahead = [scores(chunks[0]), scores(chunks[1])]
    ssq_c = jnp.zeros((ts, 1), f32)
    previous = None
    for c in range(n_conv):
        projected = []
        for i, off in ((2 * c, OFF_CC), (2 * c + 1, OFF_CU)):
            projected.append(proj(off + c * CONV_CHUNK, CONV_CHUNK))
            weighted_values(chunks[i], *softmax(chunks[i], s_ahead.pop(0)))
            if i + 2 < len(chunks):
                s_ahead.append(scores(chunks[i + 2]))
        projected.append(proj(OFF_CB + c * CONV_CHUNK, CONV_CHUNK))
        if previous is not None:
            ssq_c = ssq_c + conv_elementwise(c - 1, *previous)
        projected.append(proj(OFF_GC + c * CONV_CHUNK, CONV_CHUNK))
        previous = projected

    kbuf[:, 0:BLOCK, :] = kbuf[:, ts:ts + BLOCK, :]
    vtbuf[:, :, 0:BLOCK] = vtbuf[:, :, ts:ts + BLOCK]

    attn = attn_buf[...]
    attn_y = attn * gattn_ref[...] * sgabuf[...]
    split = (n_conv - 1) * CONV_CHUNK
    o_c = jnp.dot(cbuf[:, 0:split], wout_ref[0:split, :], preferred_element_type=f32)
    ssq_c = ssq_c + conv_elementwise(n_conv - 1, *previous)
    o_a = jnp.dot(attn_y, wout_ref[D_CONV:D_MIX, :], preferred_element_type=f32)
    o_c = o_c + jnp.dot(cbuf[:, split:D_CONV], wout_ref[split:D_CONV, :], preferred_element_type=f32)
    r_a = lax.rsqrt(jnp.mean(attn * attn, axis=-1, keepdims=True) + RMS_EPS)
    r_c = lax.rsqrt(ssq_c * (1.0 / D_CONV) + RMS_EPS)
    o_ref[...] = _rms(x + r_a * o_a + r_c * o_c, gfin_ref[...])


def _distance_bias():
    half = BLOCK // 2
    slopes = np.exp2(-8.0 * np.arange(1, N_Q_HEADS + 1, dtype=np.float32) / N_Q_HEADS).astype(np.float32)
    key = np.arange(BLOCK)[:, None]
    query = np.arange(BLOCK)[None, :]
    dist = np.where(key <= query, query - key, BLOCK + query - key).astype(np.float32)
    bias = (-slopes[:, None, None] * dist[None]).astype(np.float32) * np.float32(LOG2E)
    bias = bias.reshape(N_KV_HEADS, PAIRS_PER_KV, 2, BLOCK, 2, half)
    return bias.transpose(0, 2, 4, 3, 1, 5).reshape(N_KV_HEADS, 2, 2, BLOCK, PAIRS_PER_KV * half)


@jax.jit
def kernel(x, norm_in, w_in, conv_w, attn_sinks, norm_conv_out, norm_attn_out, w_out, norm_final):
    bsz, seq, d = x.shape
    assert d == D_MODEL and seq % SEQ_TILE == 0 and SEQ_TILE % BLOCK == 0
    assert w_in.shape == (1, D_MODEL, D_IN_PROJ) and w_out.shape == (1, D_MIX, D_MODEL)
    ts = SEQ_TILE
    const = lambda shape: pl.BlockSpec(shape, lambda b, j: (0,) * len(shape),
                                       pipeline_mode=pl.Buffered(1))
    return pl.pallas_call(
        _layer_kernel,
        grid=(bsz, seq // ts),
        in_specs=[
            pl.BlockSpec((None, ts, D_MODEL), lambda b, j: (b, j, 0)),
            const((1, D_MODEL)),
            const((D_MODEL, D_IN_PROJ)),
            const((CONV_WIDTH, 1, D_CONV)),
            pl.BlockSpec(memory_space=pltpu.SMEM),
            const((1, D_CONV)),
            const((1, D_ATTN)),
            const((D_MIX, D_MODEL)),
            const((1, D_MODEL)),
            const((N_KV_HEADS, 2, 2, BLOCK, PAIRS_PER_KV * BLOCK // 2)),
        ],
        out_specs=pl.BlockSpec((None, ts, D_MODEL), lambda b, j: (b, j, 0)),
        out_shape=jax.ShapeDtypeStruct(x.shape, x.dtype),
        scratch_shapes=[
            pltpu.VMEM((SUBLANES + ts, D_CONV), jnp.float32),
            pltpu.VMEM((4, BLOCK + ts, LANES), jnp.bfloat16),
            pltpu.VMEM((4, LANES, BLOCK + ts), jnp.bfloat16),
            pltpu.VMEM((ts, D_ATTN), jnp.bfloat16),
            pltpu.VMEM((ts, D_ATTN), jnp.float32),
            pltpu.VMEM((ts, D_CONV), jnp.float32),
            pltpu.VMEM((ts, D_ATTN), jnp.float32),
        ],
        compiler_params=pltpu.CompilerParams(
            dimension_semantics=("arbitrary", "arbitrary"),
            vmem_limit_bytes=VMEM_LIMIT_BYTES),
        name="hybrid_layer",
    )(x, norm_in, w_in[0], conv_w.reshape(CONV_WIDTH, 1, D_CONV), attn_sinks, norm_conv_out, norm_attn_out,
      w_out[0], norm_final[None, :], _distance_bias())
```

```python
import jax
import jax.numpy as jnp
import numpy as np
from jax import lax
from jax.experimental import pallas as pl
from jax.experimental.pallas import tpu as pltpu

D_MODEL = 1024
D_CONV = 1024
CONV_WIDTH = 3
N_Q_HEADS = 16
N_KV_HEADS = 2
HEAD_DIM = 64
Q_PER_KV = N_Q_HEADS // N_KV_HEADS
D_ATTN = N_Q_HEADS * HEAD_DIM
D_KV = N_KV_HEADS * HEAD_DIM
BLOCK = 128
D_MIX = D_CONV + D_ATTN
D_IN_PROJ = 4 * D_CONV + 2 * D_ATTN + 2 * D_KV
RMS_EPS = 1e-5
LOG2E = 1.4426950408889634

OFF_CB, OFF_CC, OFF_CU, OFF_GC = 0, D_CONV, 2 * D_CONV, 3 * D_CONV
OFF_Q = 4 * D_CONV
OFF_KV = OFF_Q + D_ATTN
OFF_GA = OFF_KV + 2 * D_KV

LANES = 128
SUBLANES = 8
PAIRS_PER_KV = Q_PER_KV * HEAD_DIM // LANES
SEQ_TILE = 512
CONV_CHUNK = 256
VMEM_LIMIT_BYTES = 60 * 1024 * 1024

_NT = (((1,), (1,)), ((), ()))


def _rms(x, gain):
    return x * lax.rsqrt(jnp.mean(x * x, axis=-1, keepdims=True) + RMS_EPS) * gain


def _silu(x):
    return x * (1.0 / (1.0 + jnp.exp2(x * -LOG2E)))


def _layer_kernel(x_ref, gin_ref, win_ref, convw_ref, sinks_ref, gconv_ref, gattn_ref,
                  wout_ref, gfin_ref, bias_ref, o_ref, ubuf, kbuf, vtbuf, qbuf, attn_buf, cbuf, sgabuf):
    ts = x_ref.shape[0]
    j = pl.program_id(1)
    f32, bf16 = jnp.float32, jnp.bfloat16

    @pl.when(j == 0)
    def _():
        ubuf[0:SUBLANES, :] = jnp.zeros((SUBLANES, D_CONV), f32)
        kbuf[:, 0:BLOCK, :] = jnp.zeros((4, BLOCK, LANES), bf16)
        vtbuf[:, :, 0:BLOCK] = jnp.zeros((4, LANES, BLOCK), bf16)

    x = x_ref[...]
    h = _rms(x, gin_ref[...]).astype(bf16)

    def mixed_dot(lhs, rhs_f32):
        return lax.dot_general(lhs, rhs_f32, (((1,), (0,)), ((), ())), preferred_element_type=f32)

    def proj(off, width):
        return mixed_dot(h, win_ref[:, off:off + width])

    qbuf[...] = (proj(OFF_Q, D_ATTN) * (HEAD_DIM ** -0.5 * LOG2E)).astype(bf16)
    kv = proj(OFF_KV, 2 * D_KV)
    k = kv[:, :D_KV]
    lo = lax.broadcasted_iota(jnp.int32, (ts, LANES), 1) < HEAD_DIM
    k_swapped = pltpu.roll(k, HEAD_DIM, axis=1)
    zero = jnp.zeros_like(k)
    for i, kk in enumerate([jnp.where(lo, k, zero), jnp.where(lo, zero, k_swapped),
                            jnp.where(lo, k_swapped, zero), jnp.where(lo, zero, k)]):
        kbuf[i, BLOCK:BLOCK + ts, :] = kk.astype(bf16)
    vt = kv[:, D_KV:].T.astype(bf16)
    vt0, vt1 = vt[:HEAD_DIM, :], vt[HEAD_DIM:, :]
    zrows = jnp.zeros_like(vt0)
    for i, vv in enumerate([(vt0, zrows), (zrows, vt0), (vt1, zrows), (zrows, vt1)]):
        vtbuf[i, :, BLOCK:BLOCK + ts] = jnp.concatenate(vv, axis=0)

    HALF = BLOCK // 2
    half_lanes = PAIRS_PER_KV * HALF
    kk = lax.broadcasted_iota(jnp.int32, (HALF, half_lanes), 0)
    qq = lax.broadcasted_iota(jnp.int32, (HALF, half_lanes), 1) & (HALF - 1)
    tri = kk <= qq
    lane_in_half = lax.broadcasted_iota(jnp.int32, (1, half_lanes), 1)
    neg = jnp.where(j == 0, -jnp.inf, 0.0).astype(f32)

    chunks = [(qb, g) for qb in range(ts // BLOCK) for g in range(N_KV_HEADS)]

    def scores(chunk):
        qb, g = chunk
        r0, t0 = qb * BLOCK, g * PAIRS_PER_KV
        out = []
        for half in range(2):
            q0 = r0 + half * HALF
            qs = jnp.concatenate(
                [qbuf[q0:q0 + HALF, (t0 + jp) * LANES:(t0 + jp + 1) * LANES] for jp in range(PAIRS_PER_KV)],
                axis=0)
            k0 = r0 + half * HALF
            kcat = jnp.concatenate([kbuf[2 * g, k0:k0 + 3 * HALF, :],
                                    kbuf[2 * g + 1, k0:k0 + 3 * HALF, :]], axis=0)
            out.append(lax.dot_general(kcat, qs, _NT, preferred_element_type=f32))
        return out

    def softmax(chunk, s_halves):
        qb, g = chunk
        p_cols, inv_sums = [], [[None, None], [None, None]]
        for half, s_all in enumerate(s_halves):
            p_rows = []
            for par in range(2):
                base = par * 3 * HALF
                if half == 0:
                    prev_lo, prev_hi = s_all[base:base + HALF], s_all[base + HALF:base + 2 * HALF]
                    cur_lo = s_all[base + 2 * HALF:base + 3 * HALF]
                    if qb == 0:
                        prev_lo, prev_hi = prev_lo + neg, prev_hi + neg
                    s = jnp.concatenate([jnp.where(tri, cur_lo, prev_lo), prev_hi], axis=0)
                else:
                    prev_hi = s_all[base:base + HALF]
                    cur_lo, cur_hi = s_all[base + HALF:base + 2 * HALF], s_all[base + 2 * HALF:base + 3 * HALF]
                    if qb == 0:
                        prev_hi = prev_hi + neg
                    s = jnp.concatenate([cur_lo, jnp.where(tri, cur_hi, prev_hi)], axis=0)
                s = s + bias_ref[g, par, half]
                sink = jnp.full((1, half_lanes), sinks_ref[0, g * Q_PER_KV + par] * LOG2E, f32)
                for jp in range(1, PAIRS_PER_KV):
                    sink = jnp.where(lane_in_half >= jp * HALF, sinks_ref[0, g * Q_PER_KV + 2 * jp + par] * LOG2E, sink)
                m = jnp.maximum(jnp.max(s, axis=0, keepdims=True), sink)
                p = jnp.exp2(s - m)
                inv_sums[par][half] = 1.0 / (jnp.sum(p, axis=0, keepdims=True) + jnp.exp2(sink - m))
                p_lo, p_hi = p[:HALF], p[HALF:]
                zero_p = jnp.zeros_like(p_lo)
                if half == 0:
                    p_prev = [jnp.where(tri, zero_p, p_lo), p_hi]
                    p_cur = [jnp.where(tri, p_lo, zero_p), zero_p]
                else:
                    p_prev = [zero_p, jnp.where(tri, zero_p, p_hi)]
                    p_cur = [p_lo, jnp.where(tri, p_hi, zero_p)]
                p_rows += p_prev + p_cur
            p_cols.append(jnp.concatenate(p_rows, axis=0).astype(bf16))
        inv = [jnp.concatenate(inv_sums[par], axis=1) for par in range(2)]
        return jnp.concatenate(p_cols, axis=1), inv

    def weighted_values(chunk, p_all, inv_sums):
        qb, g = chunk
        r0, t0 = qb * BLOCK, g * PAIRS_PER_KV
        vtcat = jnp.concatenate([vtbuf[2 * g, :, r0:r0 + 2 * BLOCK],
                                 vtbuf[2 * g + 1, :, r0:r0 + 2 * BLOCK]], axis=1)
        o_all = jnp.dot(vtcat, p_all, preferred_element_type=f32)
        scale = jnp.concatenate([jnp.broadcast_to(r, (HEAD_DIM, r.shape[1])) for r in inv_sums], axis=0)
        o_all = o_all * scale
        for half in range(2):
            o_half = o_all[:, half * half_lanes:(half + 1) * half_lanes].T
            for jp in range(PAIRS_PER_KV):
                attn_buf[r0 + half * HALF:r0 + (half + 1) * HALF, (t0 + jp) * LANES:(t0 + jp + 1) * LANES] = \
                    o_half[jp * HALF:(jp + 1) * HALF, :]

    def conv_elementwise(c, cc, cu, cb, gc):
        cs = slice(c * CONV_CHUNK, (c + 1) * CONV_CHUNK)
        u = cc * cu
        ubuf[SUBLANES:SUBLANES + ts, cs] = u
        u1 = ubuf[SUBLANES - 1:SUBLANES - 1 + ts, cs]
        u2 = ubuf[SUBLANES - 2:SUBLANES - 2 + ts, cs]
        ubuf[0:SUBLANES, cs] = ubuf[ts:ts + SUBLANES, cs]
        y = cb * (convw_ref[2, :, cs] * u + convw_ref[1, :, cs] * u1 + convw_ref[0, :, cs] * u2)
        cbuf[:, cs] = y * gconv_ref[:, cs] * _silu(gc)
        return jnp.sum(y * y, axis=-1, keepdims=True)

    n_conv = D_CONV // CONV_CHUNK
    assert len(chunks) == 2 * n_conv
    sgabuf[...] = _silu(proj(OFF_GA, D_ATTN))
    s_ahead = [scores(chunks[0]), scores(chunks[1])]
    ssq_c = jnp.zeros((ts, 1), f32)
    previous = None
    for c in range(n_conv):
        projected = []
        for i, off in ((2 * c, OFF_CC), (2 * c + 1, OFF_CU)):
            projected.append(proj(off + c * CONV_CHUNK, CONV_CHUNK))
            weighted_values(chunks[i], *softmax(chunks[i], s_ahead.pop(0)))
            if i + 2 < len(chunks):
                s_ahead.append(scores(chunks[i + 2]))
        projected.append(proj(OFF_CB + c * CONV_CHUNK, CONV_CHUNK))
        if previous is not None:
            ssq_c = ssq_c + conv_elementwise(c - 1, *previous)
        projected.append(proj(OFF_GC + c * CONV_CHUNK, CONV_CHUNK))
        previous = projected

    kbuf[:, 0:BLOCK, :] = kbuf[:, ts:ts + BLOCK, :]
    vtbuf[:, :, 0:BLOCK] = vtbuf[:, :, ts:ts + BLOCK]

    attn = attn_buf[...]
    attn_y = attn * gattn_ref[...] * sgabuf[...]
    split = (n_conv - 1) * CONV_CHUNK
    o_c = mixed_dot(cbuf[:, 0:split], wout_ref[0:split, :])
    ssq_c = ssq_c + conv_elementwise(n_conv - 1, *previous)
    o_a = mixed_dot(attn_y, wout_ref[D_CONV:D_MIX, :])
    o_c = o_c + mixed_dot(cbuf[:, split:D_CONV], wout_ref[split:D_CONV, :])
    r_a = lax.rsqrt(jnp.mean(attn * attn, axis=-1, keepdims=True) + RMS_EPS)
    r_c = lax.rsqrt(ssq_c * (1.0 / D_CONV) + RMS_EPS)
    o_ref[...] = _rms(x + r_a * o_a + r_c * o_c, gfin_ref[...])


def _distance_bias():
    half = BLOCK // 2
    slopes = np.exp2(-8.0 * np.arange(1, N_Q_HEADS + 1, dtype=np.float32) / N_Q_HEADS).astype(np.float32)
    key = np.arange(BLOCK)[:, None]
    query = np.arange(BLOCK)[None, :]
    dist = np.where(key <= query, query - key, BLOCK + query - key).astype(np.float32)
    bias = (-slopes[:, None, None] * dist[None]).astype(np.float32) * np.float32(LOG2E)
    bias = bias.reshape(N_KV_HEADS, PAIRS_PER_KV, 2, BLOCK, 2, half)
    return bias.transpose(0, 2, 4, 3, 1, 5).reshape(N_KV_HEADS, 2, 2, BLOCK, PAIRS_PER_KV * half)


@jax.jit
def kernel(x, norm_in, w_in, conv_w, attn_sinks, norm_conv_out, norm_attn_out, w_out, norm_final):
    bsz, seq, d = x.shape
    assert d == D_MODEL and seq % SEQ_TILE == 0 and SEQ_TILE % BLOCK == 0
    assert w_in.shape == (1, D_MODEL, D_IN_PROJ) and w_out.shape == (1, D_MIX, D_MODEL)
    ts = SEQ_TILE
    const = lambda shape: pl.BlockSpec(shape, lambda b, j: (0,) * len(shape),
                                       pipeline_mode=pl.Buffered(1))
    return pl.pallas_call(
        _layer_kernel,
        grid=(bsz, seq // ts),
        in_specs=[
            pl.BlockSpec((None, ts, D_MODEL), lambda b, j: (b, j, 0)),
            const((1, D_MODEL)),
            const((D_MODEL, D_IN_PROJ)),
            const((CONV_WIDTH, 1, D_CONV)),
            pl.BlockSpec(memory_space=pltpu.SMEM),
            const((1, D_CONV)),
            const((1, D_ATTN)),
            const((D_MIX, D_MODEL)),
            const((1, D_MODEL)),
            const((N_KV_HEADS, 2, 2, BLOCK, PAIRS_PER_KV * BLOCK // 2)),
        ],
        out_specs=pl.BlockSpec((None, ts, D_MODEL), lambda b, j: (b, j, 0)),
        out_shape=jax.ShapeDtypeStruct(x.shape, x.dtype),
        scratch_shapes=[
            pltpu.VMEM((SUBLANES + ts, D_CONV), jnp.float32),
            pltpu.VMEM((4, BLOCK + ts, LANES), jnp.bfloat16),
            pltpu.VMEM((4, LANES, BLOCK + ts), jnp.bfloat16),
            pltpu.VMEM((ts, D_ATTN), jnp.bfloat16),
            pltpu.VMEM((ts, D_ATTN), jnp.float32),
            pltpu.VMEM((ts, D_CONV), jnp.float32),
            pltpu.VMEM((ts, D_ATTN), jnp.float32),
        ],
        compiler_params=pltpu.CompilerParams(
            dimension_semantics=("arbitrary", "arbitrary"),
            vmem_limit_bytes=VMEM_LIMIT_BYTES),
        name="hybrid_layer",
    )(x, norm_in, w_in[0], conv_w.reshape(CONV_WIDTH, 1, D_CONV), attn_sinks, norm_conv_out, norm_attn_out,
      w_out[0], norm_final[None, :], _distance_bias())
```

```python
import functools

import jax
import jax.numpy as jnp
import numpy as np
from jax import lax
from jax.experimental import pallas as pl
from jax.experimental.pallas import tpu as pltpu

D_MODEL = 1024
D_CONV = 1024
CONV_WIDTH = 3
N_Q_HEADS = 16
N_KV_HEADS = 2
HEAD_DIM = 64
Q_PER_KV = N_Q_HEADS // N_KV_HEADS
D_ATTN = N_Q_HEADS * HEAD_DIM
D_KV = N_KV_HEADS * HEAD_DIM
BLOCK = 128
D_MIX = D_CONV + D_ATTN
D_IN_PROJ = 4 * D_CONV + 2 * D_ATTN + 2 * D_KV
RMS_EPS = 1e-5
LOG2E = 1.4426950408889634

OFF_CB, OFF_CC, OFF_CU, OFF_GC = 0, D_CONV, 2 * D_CONV, 3 * D_CONV
OFF_Q = 4 * D_CONV
OFF_KV = OFF_Q + D_ATTN
OFF_GA = OFF_KV + 2 * D_KV

LANES = 128
SUBLANES = 8
PAIRS_PER_KV = Q_PER_KV * HEAD_DIM // LANES
SEQ_TILE = 512
CONV_CHUNK = 256
VMEM_LIMIT_BYTES = 60 * 1024 * 1024

_NT = (((1,), (1,)), ((), ()))


def _rms(x, gain):
    return x * lax.rsqrt(jnp.mean(x * x, axis=-1, keepdims=True) + RMS_EPS) * gain


def _silu(x):
    return x * (1.0 / (1.0 + jnp.exp2(x * -LOG2E)))


def _after(value, anchor):
    return jnp.where(anchor > jnp.inf, jnp.zeros_like(value), value)


def _layer_kernel(x_ref, gin_ref, win_ref, convw_ref, sinks_ref, gconv_ref, gattn_ref,
                  wout_ref, gfin_ref, bias_ref, o_ref, xrbuf, ubuf, kbuf, vtbuf, qbuf, attn_buf, cbuf, sgabuf,
                  *, n_tiles, tiles_per_seq):
    t = pl.program_id(0)

    @pl.when(t == 0)
    def _():
        xrbuf[...] = jnp.zeros(xrbuf.shape, jnp.float32)

    @pl.when(t < n_tiles)
    def _():
        _tile_body(t % tiles_per_seq, x_ref, gin_ref, win_ref, convw_ref, sinks_ref, gconv_ref, gattn_ref,
                   wout_ref, gfin_ref, bias_ref, o_ref, xrbuf, ubuf, kbuf, vtbuf, qbuf, attn_buf, cbuf, sgabuf)

    @pl.when(t == n_tiles)
    def _():
        o_ref[...] = _rms(xrbuf[...], gfin_ref[...])


def _tile_body(j, x_ref, gin_ref, win_ref, convw_ref, sinks_ref, gconv_ref, gattn_ref,
               wout_ref, gfin_ref, bias_ref, o_ref, xrbuf, ubuf, kbuf, vtbuf, qbuf, attn_buf, cbuf, sgabuf):
    ts = x_ref.shape[0]
    f32, bf16 = jnp.float32, jnp.bfloat16

    @pl.when(j == 0)
    def _():
        ubuf[0:SUBLANES, :] = jnp.zeros((SUBLANES, D_CONV), f32)
        kbuf[:, 0:BLOCK, :] = jnp.zeros((4, BLOCK, LANES), bf16)
        vtbuf[:, :, 0:BLOCK] = jnp.zeros((4, LANES, BLOCK), bf16)

    x = x_ref[...]
    h = _rms(x, gin_ref[...]).astype(bf16)

    def mixed_dot(lhs, rhs_f32):
        return lax.dot_general(lhs, rhs_f32, (((1,), (0,)), ((), ())), preferred_element_type=f32)

    def proj(off, width):
        return mixed_dot(h, win_ref[:, off:off + width])

    q = proj(OFF_Q, D_ATTN) * (HEAD_DIM ** -0.5 * LOG2E)
    out_prev = _rms(xrbuf[...], gfin_ref[...])
    o_ref[...] = out_prev
    qbuf[...] = _after(q, out_prev).astype(bf16)
    kv = proj(OFF_KV, 2 * D_KV)
    k = kv[:, :D_KV]
    lo = lax.broadcasted_iota(jnp.int32, (ts, LANES), 1) < HEAD_DIM
    k_swapped = pltpu.roll(k, HEAD_DIM, axis=1)
    zero = jnp.zeros_like(k)
    for i, kk in enumerate([jnp.where(lo, k, zero), jnp.where(lo, zero, k_swapped),
                            jnp.where(lo, k_swapped, zero), jnp.where(lo, zero, k)]):
        kbuf[i, BLOCK:BLOCK + ts, :] = kk.astype(bf16)
    vt = kv[:, D_KV:].T.astype(bf16)
    vt0, vt1 = vt[:HEAD_DIM, :], vt[HEAD_DIM:, :]
    zrows = jnp.zeros_like(vt0)
    for i, vv in enumerate([(vt0, zrows), (zrows, vt0), (vt1, zrows), (zrows, vt1)]):
        vtbuf[i, :, BLOCK:BLOCK + ts] = jnp.concatenate(vv, axis=0)

    HALF = BLOCK // 2
    half_lanes = PAIRS_PER_KV * HALF
    kk = lax.broadcasted_iota(jnp.int32, (HALF, half_lanes), 0)
    qq = lax.broadcasted_iota(jnp.int32, (HALF, half_lanes), 1) & (HALF - 1)
    tri = kk <= qq
    lane_in_half = lax.broadcasted_iota(jnp.int32, (1, half_lanes), 1)
    neg = jnp.where(j == 0, -jnp.inf, 0.0).astype(f32)

    chunks = [(qb, g) for qb in range(ts // BLOCK) for g in range(N_KV_HEADS)]

    def scores(chunk):
        qb, g = chunk
        r0, t0 = qb * BLOCK, g * PAIRS_PER_KV
        out = []
        for half in range(2):
            q0 = r0 + half * HALF
            qs = jnp.concatenate(
                [qbuf[q0:q0 + HALF, (t0 + jp) * LANES:(t0 + jp + 1) * LANES] for jp in range(PAIRS_PER_KV)],
                axis=0)
            k0 = r0 + half * HALF
            kcat = jnp.concatenate([kbuf[2 * g, k0:k0 + 3 * HALF, :],
                                    kbuf[2 * g + 1, k0:k0 + 3 * HALF, :]], axis=0)
            out.append(lax.dot_general(kcat, qs, _NT, preferred_element_type=f32))
        return out

    def softmax(chunk, s_halves):
        qb, g = chunk
        p_cols, inv_sums = [], [[None, None], [None, None]]
        for half, s_all in enumerate(s_halves):
            p_rows = []
            for par in range(2):
                base = par * 3 * HALF
                if half == 0:
                    prev_lo, prev_hi = s_all[base:base + HALF], s_all[base + HALF:base + 2 * HALF]
                    cur_lo = s_all[base + 2 * HALF:base + 3 * HALF]
                    if qb == 0:
                        prev_lo, prev_hi = prev_lo + neg, prev_hi + neg
                    s = jnp.concatenate([jnp.where(tri, cur_lo, prev_lo), prev_hi], axis=0)
                else:
                    prev_hi = s_all[base:base + HALF]
                    cur_lo, cur_hi = s_all[base + HALF:base + 2 * HALF], s_all[base + 2 * HALF:base + 3 * HALF]
                    if qb == 0:
                        prev_hi = prev_hi + neg
                    s = jnp.concatenate([cur_lo, jnp.where(tri, cur_hi, prev_hi)], axis=0)
                s = s + bias_ref[g, par, half]
                sink = jnp.full((1, half_lanes), sinks_ref[0, g * Q_PER_KV + par] * LOG2E, f32)
                for jp in range(1, PAIRS_PER_KV):
                    sink = jnp.where(lane_in_half >= jp * HALF, sinks_ref[0, g * Q_PER_KV + 2 * jp + par] * LOG2E, sink)
                m = jnp.maximum(jnp.max(s, axis=0, keepdims=True), sink)
                p = jnp.exp2(s - m)
                inv_sums[par][half] = 1.0 / (jnp.sum(p, axis=0, keepdims=True) + jnp.exp2(sink - m))
                p_lo, p_hi = p[:HALF], p[HALF:]
                zero_p = jnp.zeros_like(p_lo)
                if half == 0:
                    p_prev = [jnp.where(tri, zero_p, p_lo), p_hi]
                    p_cur = [jnp.where(tri, p_lo, zero_p), zero_p]
                else:
                    p_prev = [zero_p, jnp.where(tri, zero_p, p_hi)]
                    p_cur = [p_lo, jnp.where(tri, p_hi, zero_p)]
                p_rows += p_prev + p_cur
            p_cols.append(jnp.concatenate(p_rows, axis=0).astype(bf16))
        inv = [jnp.concatenate(inv_sums[par], axis=1) for par in range(2)]
        return jnp.concatenate(p_cols, axis=1), inv

    def weighted_values(chunk, p_all, inv_sums):
        qb, g = chunk
        r0, t0 = qb * BLOCK, g * PAIRS_PER_KV
        vtcat = jnp.concatenate([vtbuf[2 * g, :, r0:r0 + 2 * BLOCK],
                                 vtbuf[2 * g + 1, :, r0:r0 + 2 * BLOCK]], axis=1)
        o_all = jnp.dot(vtcat, p_all, preferred_element_type=f32)
        scale = jnp.concatenate([jnp.broadcast_to(r, (HEAD_DIM, r.shape[1])) for r in inv_sums], axis=0)
        o_all = o_all * scale
        for half in range(2):
            o_half = o_all[:, half * half_lanes:(half + 1) * half_lanes].T
            for jp in range(PAIRS_PER_KV):
                attn_buf[r0 + half * HALF:r0 + (half + 1) * HALF, (t0 + jp) * LANES:(t0 + jp + 1) * LANES] = \
                    o_half[jp * HALF:(jp + 1) * HALF, :]

    def conv_elementwise(c, cc, cu, cb, gc):
        cs = slice(c * CONV_CHUNK, (c + 1) * CONV_CHUNK)
        u = cc * cu
        ubuf[SUBLANES:SUBLANES + ts, cs] = u
        u1 = ubuf[SUBLANES - 1:SUBLANES - 1 + ts, cs]
        u2 = ubuf[SUBLANES - 2:SUBLANES - 2 + ts, cs]
        ubuf[0:SUBLANES, cs] = ubuf[ts:ts + SUBLANES, cs]
        y = cb * (convw_ref[2, :, cs] * u + convw_ref[1, :, cs] * u1 + convw_ref[0, :, cs] * u2)
        cbuf[:, cs] = y * gconv_ref[:, cs] * _silu(gc)
        return jnp.sum(y * y, axis=-1, keepdims=True)

    n_conv = D_CONV // CONV_CHUNK
    assert len(chunks) == 2 * n_conv
    sgabuf[...] = _silu(proj(OFF_GA, D_ATTN))
    s_ahead = [scores(chunks[0]), scores(chunks[1])]
    ssq_c = jnp.zeros((ts, 1), f32)
    previous = None
    for c in range(n_conv):
        projected = []
        for i, off in ((2 * c, OFF_CC), (2 * c + 1, OFF_CU)):
            projected.append(proj(off + c * CONV_CHUNK, CONV_CHUNK))
            weighted_values(chunks[i], *softmax(chunks[i], s_ahead.pop(0)))
            if i + 2 < len(chunks):
                s_ahead.append(scores(chunks[i + 2]))
        projected.append(proj(OFF_CB + c * CONV_CHUNK, CONV_CHUNK))
        if previous is not None:
            ssq_c = ssq_c + conv_elementwise(c - 1, *previous)
        projected.append(proj(OFF_GC + c * CONV_CHUNK, CONV_CHUNK))
        previous = projected

    kbuf[:, 0:BLOCK, :] = kbuf[:, ts:ts + BLOCK, :]
    vtbuf[:, :, 0:BLOCK] = vtbuf[:, :, ts:ts + BLOCK]

    attn = attn_buf[...]
    attn_y = attn * gattn_ref[...] * sgabuf[...]
    split = (n_conv - 1) * CONV_CHUNK
    o_c = mixed_dot(cbuf[:, 0:split], wout_ref[0:split, :])
    ssq_c = ssq_c + conv_elementwise(n_conv - 1, *previous)
    o_a = mixed_dot(attn_y, wout_ref[D_CONV:D_MIX, :])
    o_c = o_c + mixed_dot(cbuf[:, split:D_CONV], wout_ref[split:D_CONV, :])
    r_a = lax.rsqrt(jnp.mean(attn * attn, axis=-1, keepdims=True) + RMS_EPS)
    r_c = lax.rsqrt(ssq_c * (1.0 / D_CONV) + RMS_EPS)
    xrbuf[...] = x + r_a * o_a + r_c * o_c


def _distance_bias():
    half = BLOCK // 2
    slopes = np.exp2(-8.0 * np.arange(1, N_Q_HEADS + 1, dtype=np.float32) / N_Q_HEADS).astype(np.float32)
    key = np.arange(BLOCK)[:, None]
    query = np.arange(BLOCK)[None, :]
    dist = np.where(key <= query, query - key, BLOCK + query - key).astype(np.float32)
    bias = (-slopes[:, None, None] * dist[None]).astype(np.float32) * np.float32(LOG2E)
    bias = bias.reshape(N_KV_HEADS, PAIRS_PER_KV, 2, BLOCK, 2, half)
    return bias.transpose(0, 2, 4, 3, 1, 5).reshape(N_KV_HEADS, 2, 2, BLOCK, PAIRS_PER_KV * half)


@jax.jit
def kernel(x, norm_in, w_in, conv_w, attn_sinks, norm_conv_out, norm_attn_out, w_out, norm_final):
    bsz, seq, d = x.shape
    assert d == D_MODEL and seq % SEQ_TILE == 0 and SEQ_TILE % BLOCK == 0
    assert w_in.shape == (1, D_MODEL, D_IN_PROJ) and w_out.shape == (1, D_MIX, D_MODEL)
    ts = SEQ_TILE
    tiles_per_seq = seq // ts
    n_tiles = bsz * tiles_per_seq

    def x_tile(shift):
        def index_map(t):
            i = jnp.clip(t + shift, 0, n_tiles - 1)
            return (i // tiles_per_seq, i % tiles_per_seq, 0)
        return pl.BlockSpec((None, ts, D_MODEL), index_map)

    const = lambda shape: pl.BlockSpec(shape, lambda t: (0,) * len(shape), pipeline_mode=pl.Buffered(1))
    return pl.pallas_call(
        functools.partial(_layer_kernel, n_tiles=n_tiles, tiles_per_seq=tiles_per_seq),
        grid=(n_tiles + 1,),
        in_specs=[
            x_tile(0),
            const((1, D_MODEL)),
            const((D_MODEL, D_IN_PROJ)),
            const((CONV_WIDTH, 1, D_CONV)),
            pl.BlockSpec(memory_space=pltpu.SMEM),
            const((1, D_CONV)),
            const((1, D_ATTN)),
            const((D_MIX, D_MODEL)),
            const((1, D_MODEL)),
            const((N_KV_HEADS, 2, 2, BLOCK, PAIRS_PER_KV * BLOCK // 2)),
        ],
        out_specs=x_tile(-1),
        out_shape=jax.ShapeDtypeStruct(x.shape, x.dtype),
        scratch_shapes=[
            pltpu.VMEM((ts, D_MODEL), jnp.float32),
            pltpu.VMEM((SUBLANES + ts, D_CONV), jnp.float32),
            pltpu.VMEM((4, BLOCK + ts, LANES), jnp.bfloat16),
            pltpu.VMEM((4, LANES, BLOCK + ts), jnp.bfloat16),
            pltpu.VMEM((ts, D_ATTN), jnp.bfloat16),
            pltpu.VMEM((ts, D_ATTN), jnp.float32),
            pltpu.VMEM((ts, D_CONV), jnp.float32),
            pltpu.VMEM((ts, D_ATTN), jnp.float32),
        ],
        compiler_params=pltpu.CompilerParams(
            dimension_semantics=("arbitrary",),
            vmem_limit_bytes=VMEM_LIMIT_BYTES),
        name="hybrid_layer",
    )(x, norm_in, w_in[0], conv_w.reshape(CONV_WIDTH, 1, D_CONV), attn_sinks, norm_conv_out, norm_attn_out,
      w_out[0], norm_final[None, :], _distance_bias())
```

```python
import jax
import jax.numpy as jnp
import numpy as np
from jax import lax
from jax.experimental import pallas as pl
from jax.experimental.pallas import tpu as pltpu

D_MODEL = 1024
D_CONV = 1024
CONV_WIDTH = 3
N_Q_HEADS = 16
N_KV_HEADS = 2
HEAD_DIM = 64
Q_PER_KV = N_Q_HEADS // N_KV_HEADS
D_ATTN = N_Q_HEADS * HEAD_DIM
D_KV = N_KV_HEADS * HEAD_DIM
BLOCK = 128
D_MIX = D_CONV + D_ATTN
D_IN_PROJ = 4 * D_CONV + 2 * D_ATTN + 2 * D_KV
RMS_EPS = 1e-5
LOG2E = 1.4426950408889634

OFF_CB, OFF_CC, OFF_CU, OFF_GC = 0, D_CONV, 2 * D_CONV, 3 * D_CONV
OFF_Q = 4 * D_CONV
OFF_KV = OFF_Q + D_ATTN
OFF_GA = OFF_KV + 2 * D_KV

LANES = 128
SUBLANES = 8
PAIRS_PER_KV = Q_PER_KV * HEAD_DIM // LANES
KV_VARIANTS = 2 * N_KV_HEADS
SEQ_TILE = 512
CONV_CHUNK = 256
VMEM_LIMIT_BYTES = 60 * 1024 * 1024

_NT = (((1,), (1,)), ((), ()))


def _rms(x, gain):
    return x * lax.rsqrt(jnp.mean(x * x, axis=-1, keepdims=True) + RMS_EPS) * gain


def _silu(x):
    return x * (1.0 / (1.0 + jnp.exp2(x * -LOG2E)))


def _layer_kernel(x_ref, gin_ref, win_ref, convw_ref, sinks_ref, gconv_ref, gattn_ref,
                  wout_ref, gfin_ref, bias_ref, o_ref, ubuf, kbuf, vtbuf, qbuf, attn_buf, cbuf, sgabuf):
    ts = x_ref.shape[0]
    j = pl.program_id(1)
    f32, bf16 = jnp.float32, jnp.bfloat16

    @pl.when(j == 0)
    def _():
        ubuf[0:SUBLANES, :] = jnp.zeros((SUBLANES, D_CONV), f32)
        kbuf[:, 0:BLOCK, :] = jnp.zeros((KV_VARIANTS, BLOCK, LANES), bf16)
        vtbuf[:, :, 0:BLOCK] = jnp.zeros((KV_VARIANTS, LANES, BLOCK), bf16)

    x = x_ref[...]
    h = _rms(x, gin_ref[...]).astype(bf16)

    def mixed_dot(lhs, rhs_f32):
        return lax.dot_general(lhs, rhs_f32, (((1,), (0,)), ((), ())), preferred_element_type=f32)

    def proj(off, width):
        return mixed_dot(h, win_ref[:, off:off + width])

    qbuf[...] = (proj(OFF_Q, D_ATTN) * (HEAD_DIM ** -0.5 * LOG2E)).astype(bf16)
    kv = proj(OFF_KV, 2 * D_KV)
    k = kv[:, :D_KV]
    lo = lax.broadcasted_iota(jnp.int32, (ts, LANES), 1) < HEAD_DIM
    k_swapped = pltpu.roll(k, HEAD_DIM, axis=1)
    zero = jnp.zeros_like(k)
    for i, kk in enumerate([jnp.where(lo, k, zero), jnp.where(lo, zero, k_swapped),
                            jnp.where(lo, k_swapped, zero), jnp.where(lo, zero, k)]):
        kbuf[i, BLOCK:BLOCK + ts, :] = kk.astype(bf16)
    vt = kv[:, D_KV:].T.astype(bf16)
    vt0, vt1 = vt[:HEAD_DIM, :], vt[HEAD_DIM:, :]
    zrows = jnp.zeros_like(vt0)
    for i, vv in enumerate([(vt0, zrows), (zrows, vt0), (vt1, zrows), (zrows, vt1)]):
        vtbuf[i, :, BLOCK:BLOCK + ts] = jnp.concatenate(vv, axis=0)

    HALF = BLOCK // 2
    half_lanes = PAIRS_PER_KV * HALF
    kk = lax.broadcasted_iota(jnp.int32, (HALF, half_lanes), 0)
    qq = lax.broadcasted_iota(jnp.int32, (HALF, half_lanes), 1) & (HALF - 1)
    tri = kk <= qq
    lane_in_half = lax.broadcasted_iota(jnp.int32, (1, half_lanes), 1)
    neg = jnp.where(j == 0, -jnp.inf, 0.0).astype(f32)

    chunks = [(qb, g) for qb in range(ts // BLOCK) for g in range(N_KV_HEADS)]

    def scores(chunk):
        qb, g = chunk
        r0, t0 = qb * BLOCK, g * PAIRS_PER_KV
        out = []
        for half in range(2):
            q0 = r0 + half * HALF
            qs = jnp.concatenate(
                [qbuf[q0:q0 + HALF, (t0 + jp) * LANES:(t0 + jp + 1) * LANES] for jp in range(PAIRS_PER_KV)],
                axis=0)
            k0 = r0 + half * HALF
            kcat = jnp.concatenate([kbuf[2 * g, k0:k0 + 3 * HALF, :],
                                    kbuf[2 * g + 1, k0:k0 + 3 * HALF, :]], axis=0)
            out.append(lax.dot_general(kcat, qs, _NT, preferred_element_type=f32))
        return out

    def softmax(chunk, s_halves):
        qb, g = chunk
        p_cols, inv_sums = [], [[None, None], [None, None]]
        for half, s_all in enumerate(s_halves):
            p_rows = []
            for par in range(2):
                base = par * 3 * HALF
                if half == 0:
                    prev_lo, prev_hi = s_all[base:base + HALF], s_all[base + HALF:base + 2 * HALF]
                    cur_lo = s_all[base + 2 * HALF:base + 3 * HALF]
                    if qb == 0:
                        prev_lo, prev_hi = prev_lo + neg, prev_hi + neg
                    s = jnp.concatenate([jnp.where(tri, cur_lo, prev_lo), prev_hi], axis=0)
                else:
                    prev_hi = s_all[base:base + HALF]
                    cur_lo, cur_hi = s_all[base + HALF:base + 2 * HALF], s_all[base + 2 * HALF:base + 3 * HALF]
                    if qb == 0:
                        prev_hi = prev_hi + neg
                    s = jnp.concatenate([cur_lo, jnp.where(tri, cur_hi, prev_hi)], axis=0)
                s = s + bias_ref[g, par, half]
                sink = jnp.full((1, half_lanes), sinks_ref[0, g * Q_PER_KV + par] * LOG2E, f32)
                for jp in range(1, PAIRS_PER_KV):
                    sink = jnp.where(lane_in_half >= jp * HALF, sinks_ref[0, g * Q_PER_KV + 2 * jp + par] * LOG2E, sink)
                m = jnp.maximum(jnp.max(s, axis=0, keepdims=True), sink)
                p = jnp.exp2(s - m)
                inv_sums[par][half] = 1.0 / (jnp.sum(p, axis=0, keepdims=True) + jnp.exp2(sink - m))
                p_lo, p_hi = p[:HALF], p[HALF:]
                zero_p = jnp.zeros_like(p_lo)
                if half == 0:
                    p_prev = [jnp.where(tri, zero_p, p_lo), p_hi]
                    p_cur = [jnp.where(tri, p_lo, zero_p), zero_p]
                else:
                    p_prev = [zero_p, jnp.where(tri, zero_p, p_hi)]
                    p_cur = [p_lo, jnp.where(tri, p_hi, zero_p)]
                p_rows += p_prev + p_cur
            p_cols.append(jnp.concatenate(p_rows, axis=0).astype(bf16))
        inv = [jnp.concatenate(inv_sums[par], axis=1) for par in range(2)]
        return jnp.concatenate(p_cols, axis=1), inv

    def weighted_values(chunk, p_all, inv_sums):
        qb, g = chunk
        r0, t0 = qb * BLOCK, g * PAIRS_PER_KV
        vtcat = jnp.concatenate([vtbuf[2 * g, :, r0:r0 + 2 * BLOCK],
                                 vtbuf[2 * g + 1, :, r0:r0 + 2 * BLOCK]], axis=1)
        o_all = jnp.dot(vtcat, p_all, preferred_element_type=f32)
        scale = jnp.concatenate([jnp.broadcast_to(r, (HEAD_DIM, r.shape[1])) for r in inv_sums], axis=0)
        o_all = o_all * scale
        for half in range(2):
            o_half = o_all[:, half * half_lanes:(half + 1) * half_lanes].T
            for jp in range(PAIRS_PER_KV):
                attn_buf[r0 + half * HALF:r0 + (half + 1) * HALF, (t0 + jp) * LANES:(t0 + jp + 1) * LANES] = \
                    o_half[jp * HALF:(jp + 1) * HALF, :]

    def conv_elementwise(c, cc, cu, cb, gc):
        cs = slice(c * CONV_CHUNK, (c + 1) * CONV_CHUNK)
        u = cc * cu
        ubuf[SUBLANES:SUBLANES + ts, cs] = u
        u1 = ubuf[SUBLANES - 1:SUBLANES - 1 + ts, cs]
        u2 = ubuf[SUBLANES - 2:SUBLANES - 2 + ts, cs]
        ubuf[0:SUBLANES, cs] = ubuf[ts:ts + SUBLANES, cs]
        y = cb * (convw_ref[2, :, cs] * u + convw_ref[1, :, cs] * u1 + convw_ref[0, :, cs] * u2)
        cbuf[:, cs] = y * gconv_ref[:, cs] * _silu(gc)
        return jnp.sum(y * y, axis=-1, keepdims=True)

    n_conv = D_CONV // CONV_CHUNK
    assert len(chunks) == 2 * n_conv
    sgabuf[...] = _silu(proj(OFF_GA, D_ATTN))
    s_ahead = [scores(chunks[0]), scores(chunks[1])]
    ssq_c = jnp.zeros((ts, 1), f32)
    previous = None
    for c in range(n_conv):
        projected = []
        for i, off in ((2 * c, OFF_CC), (2 * c + 1, OFF_CU)):
            projected.append(proj(off + c * CONV_CHUNK, CONV_CHUNK))
            weighted_values(chunks[i], *softmax(chunks[i], s_ahead.pop(0)))
            if i + 2 < len(chunks):
                s_ahead.append(scores(chunks[i + 2]))
        projected.append(proj(OFF_CB + c * CONV_CHUNK, CONV_CHUNK))
        if previous is not None:
            ssq_c = ssq_c + conv_elementwise(c - 1, *previous)
        projected.append(proj(OFF_GC + c * CONV_CHUNK, CONV_CHUNK))
        previous = projected

    kbuf[:, 0:BLOCK, :] = kbuf[:, ts:ts + BLOCK, :]
    vtbuf[:, :, 0:BLOCK] = vtbuf[:, :, ts:ts + BLOCK]

    attn = attn_buf[...]
    attn_y = attn * gattn_ref[...] * sgabuf[...]
    split = (n_conv - 1) * CONV_CHUNK
    o_c = mixed_dot(cbuf[:, 0:split], wout_ref[0:split, :])
    ssq_c = ssq_c + conv_elementwise(n_conv - 1, *previous)
    o_a = mixed_dot(attn_y, wout_ref[D_CONV:D_MIX, :])
    o_c = o_c + mixed_dot(cbuf[:, split:D_CONV], wout_ref[split:D_CONV, :])
    r_a = lax.rsqrt(jnp.mean(attn * attn, axis=-1, keepdims=True) + RMS_EPS)
    r_c = lax.rsqrt(ssq_c * (1.0 / D_CONV) + RMS_EPS)
    o_ref[...] = _rms(x + r_a * o_a + r_c * o_c, gfin_ref[...])


def _distance_bias():
    half = BLOCK // 2
    slopes = np.exp2(-8.0 * np.arange(1, N_Q_HEADS + 1, dtype=np.float32) / N_Q_HEADS).astype(np.float32)
    key = np.arange(BLOCK)[:, None]
    query = np.arange(BLOCK)[None, :]
    dist = np.where(key <= query, query - key, BLOCK + query - key).astype(np.float32)
    bias = (-slopes[:, None, None] * dist[None]).astype(np.float32) * np.float32(LOG2E)
    bias = bias.reshape(N_KV_HEADS, PAIRS_PER_KV, 2, BLOCK, 2, half)
    return bias.transpose(0, 2, 4, 3, 1, 5).reshape(N_KV_HEADS, 2, 2, BLOCK, PAIRS_PER_KV * half)


@jax.jit
def kernel(x, norm_in, w_in, conv_w, attn_sinks, norm_conv_out, norm_attn_out, w_out, norm_final):
    bsz, seq, d = x.shape
    assert d == D_MODEL and seq % SEQ_TILE == 0 and SEQ_TILE % BLOCK == 0
    assert w_in.shape == (1, D_MODEL, D_IN_PROJ) and w_out.shape == (1, D_MIX, D_MODEL)
    ts = SEQ_TILE
    const = lambda shape: pl.BlockSpec(shape, lambda b, j: (0,) * len(shape),
                                       pipeline_mode=pl.Buffered(1))
    return pl.pallas_call(
        _layer_kernel,
        grid=(bsz, seq // ts),
        in_specs=[
            pl.BlockSpec((None, ts, D_MODEL), lambda b, j: (b, j, 0)),
            const((1, D_MODEL)),
            const((D_MODEL, D_IN_PROJ)),
            const((CONV_WIDTH, 1, D_CONV)),
            pl.BlockSpec(memory_space=pltpu.SMEM),
            const((1, D_CONV)),
            const((1, D_ATTN)),
            const((D_MIX, D_MODEL)),
            const((1, D_MODEL)),
            const((N_KV_HEADS, 2, 2, BLOCK, PAIRS_PER_KV * BLOCK // 2)),
        ],
        out_specs=pl.BlockSpec((None, ts, D_MODEL), lambda b, j: (b, j, 0)),
        out_shape=jax.ShapeDtypeStruct(x.shape, x.dtype),
        scratch_shapes=[
            pltpu.VMEM((SUBLANES + ts, D_CONV), jnp.float32),
            pltpu.VMEM((KV_VARIANTS, BLOCK + ts, LANES), jnp.bfloat16),
            pltpu.VMEM((KV_VARIANTS, LANES, BLOCK + ts), jnp.bfloat16),
            pltpu.VMEM((ts, D_ATTN), jnp.bfloat16),
            pltpu.VMEM((ts, D_ATTN), jnp.float32),
            pltpu.VMEM((ts, D_CONV), jnp.float32),
            pltpu.VMEM((ts, D_ATTN), jnp.float32),
        ],
        compiler_params=pltpu.CompilerParams(
            dimension_semantics=("arbitrary", "arbitrary"),
            vmem_limit_bytes=VMEM_LIMIT_BYTES),
        name="hybrid_layer",
    )(x, norm_in, w_in[0], conv_w.reshape(CONV_WIDTH, 1, D_CONV), attn_sinks, norm_conv_out, norm_attn_out,
      w_out[0], norm_final[None, :], _distance_bias())
```

```python
import jax
import jax.numpy as jnp
import numpy as np
from jax import lax
from jax.experimental import pallas as pl
from jax.experimental.pallas import tpu as pltpu

D_MODEL = 1024
D_CONV = 1024
CONV_WIDTH = 3
N_Q_HEADS = 16
N_KV_HEADS = 2
HEAD_DIM = 64
Q_PER_KV = N_Q_HEADS // N_KV_HEADS
D_ATTN = N_Q_HEADS * HEAD_DIM
D_KV = N_KV_HEADS * HEAD_DIM
BLOCK = 128
D_MIX = D_CONV + D_ATTN
D_IN_PROJ = 4 * D_CONV + 2 * D_ATTN + 2 * D_KV
RMS_EPS = 1e-5
LOG2E = 1.4426950408889634

OFF_CB, OFF_CC, OFF_CU, OFF_GC = 0, D_CONV, 2 * D_CONV, 3 * D_CONV
OFF_Q = 4 * D_CONV
OFF_KV = OFF_Q + D_ATTN
OFF_GA = OFF_KV + 2 * D_KV

LANES = 128
SUBLANES = 8
PAIRS_PER_KV = Q_PER_KV * HEAD_DIM // LANES
KV_VARIANTS = 2 * N_KV_HEADS
SEQ_TILE = 512
CONV_CHUNK = 256
VMEM_LIMIT_BYTES = 60 * 1024 * 1024

_NT = (((1,), (1,)), ((), ()))


def _rms(x, gain):
    return x * lax.rsqrt(jnp.mean(x * x, axis=-1, keepdims=True) + RMS_EPS) * gain


def _silu(x):
    return x * (1.0 / (1.0 + jnp.exp2(x * -LOG2E)))


def _layer_kernel(x_ref, gin_ref, win_ref, convw_ref, sinks_ref, gconv_ref, gattn_ref,
                  wout_ref, gfin_ref, bias_ref, o_ref, ubuf, kbuf, vtbuf, qbuf, attn_buf, cbuf, sgabuf):
    ts = x_ref.shape[0]
    j = pl.program_id(1)
    f32, bf16 = jnp.float32, jnp.bfloat16

    @pl.when(j == 0)
    def _():
        ubuf[0:SUBLANES, :] = jnp.zeros((SUBLANES, D_CONV), f32)
        kbuf[:, 0:BLOCK, :] = jnp.zeros((KV_VARIANTS, BLOCK, LANES), bf16)
        vtbuf[:, :, 0:BLOCK] = jnp.zeros((KV_VARIANTS, LANES, BLOCK), bf16)

    def mixed_dot(lhs, rhs_f32):
        return lax.dot_general(lhs, rhs_f32, (((1,), (0,)), ((), ())), preferred_element_type=f32)

    x = x_ref[...]
    xg = x * gin_ref[...]
    q_unscaled = mixed_dot(xg.astype(bf16), win_ref[:, OFF_Q:OFF_Q + D_ATTN])
    r_in = lax.rsqrt(jnp.mean(x * x, axis=-1, keepdims=True) + RMS_EPS)
    h = (xg * r_in).astype(bf16)

    def proj(off, width):
        return mixed_dot(h, win_ref[:, off:off + width])

    qbuf[...] = (q_unscaled * (r_in * (HEAD_DIM ** -0.5 * LOG2E))).astype(bf16)
    kv = proj(OFF_KV, 2 * D_KV)
    k = kv[:, :D_KV]
    lo = lax.broadcasted_iota(jnp.int32, (ts, LANES), 1) < HEAD_DIM
    k_swapped = pltpu.roll(k, HEAD_DIM, axis=1)
    zero = jnp.zeros_like(k)
    for i, kk in enumerate([jnp.where(lo, k, zero), jnp.where(lo, zero, k_swapped),
                            jnp.where(lo, k_swapped, zero), jnp.where(lo, zero, k)]):
        kbuf[i, BLOCK:BLOCK + ts, :] = kk.astype(bf16)
    vt = kv[:, D_KV:].T.astype(bf16)
    vt0, vt1 = vt[:HEAD_DIM, :], vt[HEAD_DIM:, :]
    zrows = jnp.zeros_like(vt0)
    for i, vv in enumerate([(vt0, zrows), (zrows, vt0), (vt1, zrows), (zrows, vt1)]):
        vtbuf[i, :, BLOCK:BLOCK + ts] = jnp.concatenate(vv, axis=0)

    HALF = BLOCK // 2
    half_lanes = PAIRS_PER_KV * HALF
    kk = lax.broadcasted_iota(jnp.int32, (HALF, half_lanes), 0)
    qq = lax.broadcasted_iota(jnp.int32, (HALF, half_lanes), 1) & (HALF - 1)
    tri = kk <= qq
    lane_in_half = lax.broadcasted_iota(jnp.int32, (1, half_lanes), 1)
    neg = jnp.where(j == 0, -jnp.inf, 0.0).astype(f32)

    chunks = [(qb, g) for qb in range(ts // BLOCK) for g in range(N_KV_HEADS)]

    def scores(chunk):
        qb, g = chunk
        r0, t0 = qb * BLOCK, g * PAIRS_PER_KV
        out = []
        for half in range(2):
            q0 = r0 + half * HALF
            qs = jnp.concatenate(
                [qbuf[q0:q0 + HALF, (t0 + jp) * LANES:(t0 + jp + 1) * LANES] for jp in range(PAIRS_PER_KV)],
                axis=0)
            k0 = r0 + half * HALF
            kcat = jnp.concatenate([kbuf[2 * g, k0:k0 + 3 * HALF, :],
                                    kbuf[2 * g + 1, k0:k0 + 3 * HALF, :]], axis=0)
            out.append(lax.dot_general(kcat, qs, _NT, preferred_element_type=f32))
        return out

    def softmax(chunk, s_halves):
        qb, g = chunk
        p_cols, inv_sums = [], [[None, None], [None, None]]
        for half, s_all in enumerate(s_halves):
            p_rows = []
            for par in range(2):
                base = par * 3 * HALF
                if half == 0:
                    prev_lo, prev_hi = s_all[base:base + HALF], s_all[base + HALF:base + 2 * HALF]
                    cur_lo = s_all[base + 2 * HALF:base + 3 * HALF]
                    if qb == 0:
                        prev_lo, prev_hi = prev_lo + neg, prev_hi + neg
                    s = jnp.concatenate([jnp.where(tri, cur_lo, prev_lo), prev_hi], axis=0)
                else:
                    prev_hi = s_all[base:base + HALF]
                    cur_lo, cur_hi = s_all[base + HALF:base + 2 * HALF], s_all[base + 2 * HALF:base + 3 * HALF]
                    if qb == 0:
                        prev_hi = prev_hi + neg
                    s = jnp.concatenate([cur_lo, jnp.where(tri, cur_hi, prev_hi)], axis=0)
                s = s + bias_ref[g, par, half]
                sink = jnp.full((1, half_lanes), sinks_ref[0, g * Q_PER_KV + par] * LOG2E, f32)
                for jp in range(1, PAIRS_PER_KV):
                    sink = jnp.where(lane_in_half >= jp * HALF, sinks_ref[0, g * Q_PER_KV + 2 * jp + par] * LOG2E, sink)
                m = jnp.maximum(jnp.max(s, axis=0, keepdims=True), sink)
                p = jnp.exp2(s - m)
                inv_sums[par][half] = 1.0 / (jnp.sum(p, axis=0, keepdims=True) + jnp.exp2(sink - m))
                p_lo, p_hi = p[:HALF], p[HALF:]
                zero_p = jnp.zeros_like(p_lo)
                if half == 0:
                    p_prev = [jnp.where(tri, zero_p, p_lo), p_hi]
                    p_cur = [jnp.where(tri, p_lo, zero_p), zero_p]
                else:
                    p_prev = [zero_p, jnp.where(tri, zero_p, p_hi)]
                    p_cur = [p_lo, jnp.where(tri, p_hi, zero_p)]
                p_rows += p_prev + p_cur
            p_cols.append(jnp.concatenate(p_rows, axis=0).astype(bf16))
        inv = [jnp.concatenate(inv_sums[par], axis=1) for par in range(2)]
        return jnp.concatenate(p_cols, axis=1), inv

    def weighted_values(chunk, p_all, inv_sums):
        qb, g = chunk
        r0, t0 = qb * BLOCK, g * PAIRS_PER_KV
        vtcat = jnp.concatenate([vtbuf[2 * g, :, r0:r0 + 2 * BLOCK],
                                 vtbuf[2 * g + 1, :, r0:r0 + 2 * BLOCK]], axis=1)
        o_all = jnp.dot(vtcat, p_all, preferred_element_type=f32)
        scale = jnp.concatenate([jnp.broadcast_to(r, (HEAD_DIM, r.shape[1])) for r in inv_sums], axis=0)
        o_all = o_all * scale
        for half in range(2):
            o_half = o_all[:, half * half_lanes:(half + 1) * half_lanes].T
            for jp in range(PAIRS_PER_KV):
                attn_buf[r0 + half * HALF:r0 + (half + 1) * HALF, (t0 + jp) * LANES:(t0 + jp + 1) * LANES] = \
                    o_half[jp * HALF:(jp + 1) * HALF, :]

    def conv_elementwise(c, cc, cu, cb, gc):
        cs = slice(c * CONV_CHUNK, (c + 1) * CONV_CHUNK)
        u = cc * cu
        ubuf[SUBLANES:SUBLANES + ts, cs] = u
        u1 = ubuf[SUBLANES - 1:SUBLANES - 1 + ts, cs]
        u2 = ubuf[SUBLANES - 2:SUBLANES - 2 + ts, cs]
        ubuf[0:SUBLANES, cs] = ubuf[ts:ts + SUBLANES, cs]
        y = cb * (convw_ref[2, :, cs] * u + convw_ref[1, :, cs] * u1 + convw_ref[0, :, cs] * u2)
        cbuf[:, cs] = y * gconv_ref[:, cs] * _silu(gc)
        return jnp.sum(y * y, axis=-1, keepdims=True)

    n_conv = D_CONV // CONV_CHUNK
    assert len(chunks) == 2 * n_conv
    sgabuf[...] = _silu(proj(OFF_GA, D_ATTN))
    s_ahead = [scores(chunks[0]), scores(chunks[1])]
    ssq_c = jnp.zeros((ts, 1), f32)
    previous = None
    for c in range(n_conv):
        projected = []
        for i, off in ((2 * c, OFF_CC), (2 * c + 1, OFF_CU)):
            projected.append(proj(off + c * CONV_CHUNK, CONV_CHUNK))
            weighted_values(chunks[i], *softmax(chunks[i], s_ahead.pop(0)))
            if i + 2 < len(chunks):
                s_ahead.append(scores(chunks[i + 2]))
        projected.append(proj(OFF_CB + c * CONV_CHUNK, CONV_CHUNK))
        if previous is not None:
            ssq_c = ssq_c + conv_elementwise(c - 1, *previous)
        projected.append(proj(OFF_GC + c * CONV_CHUNK, CONV_CHUNK))
        previous = projected

    kbuf[:, 0:BLOCK, :] = kbuf[:, ts:ts + BLOCK, :]
    vtbuf[:, :, 0:BLOCK] = vtbuf[:, :, ts:ts + BLOCK]

    attn = attn_buf[...]
    attn_y = attn * gattn_ref[...] * sgabuf[...]
    split = (n_conv - 1) * CONV_CHUNK
    o_c = mixed_dot(cbuf[:, 0:split], wout_ref[0:split, :])
    ssq_c = ssq_c + conv_elementwise(n_conv - 1, *previous)
    o_a = mixed_dot(attn_y, wout_ref[D_CONV:D_MIX, :])
    o_c = o_c + mixed_dot(cbuf[:, split:D_CONV], wout_ref[split:D_CONV, :])
    r_a = lax.rsqrt(jnp.mean(attn * attn, axis=-1, keepdims=True) + RMS_EPS)
    r_c = lax.rsqrt(ssq_c * (1.0 / D_CONV) + RMS_EPS)
    o_ref[...] = _rms(x + r_a * o_a + r_c * o_c, gfin_ref[...])


def _distance_bias():
    half = BLOCK // 2
    slopes = np.exp2(-8.0 * np.arange(1, N_Q_HEADS + 1, dtype=np.float32) / N_Q_HEADS).astype(np.float32)
    key = np.arange(BLOCK)[:, None]
    query = np.arange(BLOCK)[None, :]
    dist = np.where(key <= query, query - key, BLOCK + query - key).astype(np.float32)
    bias = (-slopes[:, None, None] * dist[None]).astype(np.float32) * np.float32(LOG2E)
    bias = bias.reshape(N_KV_HEADS, PAIRS_PER_KV, 2, BLOCK, 2, half)
    return bias.transpose(0, 2, 4, 3, 1, 5).reshape(N_KV_HEADS, 2, 2, BLOCK, PAIRS_PER_KV * half)


@jax.jit
def kernel(x, norm_in, w_in, conv_w, attn_sinks, norm_conv_out, norm_attn_out, w_out, norm_final):
    bsz, seq, d = x.shape
    assert d == D_MODEL and seq % SEQ_TILE == 0 and SEQ_TILE % BLOCK == 0
    assert w_in.shape == (1, D_MODEL, D_IN_PROJ) and w_out.shape == (1, D_MIX, D_MODEL)
    ts = SEQ_TILE
    const = lambda shape: pl.BlockSpec(shape, lambda b, j: (0,) * len(shape),
                                       pipeline_mode=pl.Buffered(1))
    return pl.pallas_call(
        _layer_kernel,
        grid=(bsz, seq // ts),
        in_specs=[
            pl.BlockSpec((None, ts, D_MODEL), lambda b, j: (b, j, 0)),
            const((1, D_MODEL)),
            const((D_MODEL, D_IN_PROJ)),
            const((CONV_WIDTH, 1, D_CONV)),
            pl.BlockSpec(memory_space=pltpu.SMEM),
            const((1, D_CONV)),
            const((1, D_ATTN)),
            const((D_MIX, D_MODEL)),
            const((1, D_MODEL)),
            const((N_KV_HEADS, 2, 2, BLOCK, PAIRS_PER_KV * BLOCK // 2)),
        ],
        out_specs=pl.BlockSpec((None, ts, D_MODEL), lambda b, j: (b, j, 0)),
        out_shape=jax.ShapeDtypeStruct(x.shape, x.dtype),
        scratch_shapes=[
            pltpu.VMEM((SUBLANES + ts, D_CONV), jnp.float32),
            pltpu.VMEM((KV_VARIANTS, BLOCK + ts, LANES), jnp.bfloat16),
            pltpu.VMEM((KV_VARIANTS, LANES, BLOCK + ts), jnp.bfloat16),
            pltpu.VMEM((ts, D_ATTN), jnp.bfloat16),
            pltpu.VMEM((ts, D_ATTN), jnp.float32),
            pltpu.VMEM((ts, D_CONV), jnp.float32),
            pltpu.VMEM((ts, D_ATTN), jnp.float32),
        ],
        compiler_params=pltpu.CompilerParams(
            dimension_semantics=("arbitrary", "arbitrary"),
            vmem_limit_bytes=VMEM_LIMIT_BYTES),
        name="hybrid_layer",
    )(x, norm_in, w_in[0], conv_w.reshape(CONV_WIDTH, 1, D_CONV), attn_sinks, norm_conv_out, norm_attn_out,
      w_out[0], norm_final[None, :], _distance_bias())
```

```python
import jax
import jax.numpy as jnp
import numpy as np
from jax import lax
from jax.experimental import pallas as pl
from jax.experimental.pallas import tpu as pltpu

D_MODEL = 1024
D_CONV = 1024
CONV_WIDTH = 3
N_Q_HEADS = 16
N_KV_HEADS = 2
HEAD_DIM = 64
Q_PER_KV = N_Q_HEADS // N_KV_HEADS
D_ATTN = N_Q_HEADS * HEAD_DIM
D_KV = N_KV_HEADS * HEAD_DIM
BLOCK = 128
D_MIX = D_CONV + D_ATTN
D_IN_PROJ = 4 * D_CONV + 2 * D_ATTN + 2 * D_KV
RMS_EPS = 1e-5
LOG2E = 1.4426950408889634

OFF_CB, OFF_CC, OFF_CU, OFF_GC = 0, D_CONV, 2 * D_CONV, 3 * D_CONV
OFF_Q = 4 * D_CONV
OFF_KV = OFF_Q + D_ATTN
OFF_GA = OFF_KV + 2 * D_KV

LANES = 128
SUBLANES = 8
PAIRS_PER_KV = Q_PER_KV * HEAD_DIM // LANES
KV_VARIANTS = 2 * N_KV_HEADS
SEQ_TILE = 512
CONV_CHUNK = 256
VMEM_LIMIT_BYTES = 60 * 1024 * 1024

_NT = (((1,), (1,)), ((), ()))


def _rms(x, gain):
    return x * lax.rsqrt(jnp.mean(x * x, axis=-1, keepdims=True) + RMS_EPS) * gain


def _silu(x):
    return x * (1.0 / (1.0 + jnp.exp2(x * -LOG2E)))


def _layer_kernel(x_ref, gin_ref, win_ref, convw_ref, sinks_ref, gconv_ref, gattn_ref,
                  wout_ref, gfin_ref, bias_ref, o_ref, ubuf, kbuf, vtbuf, qbuf, attn_buf, cbuf, sgabuf):
    ts = x_ref.shape[0]
    j = pl.program_id(1)
    f32, bf16 = jnp.float32, jnp.bfloat16

    @pl.when(j == 0)
    def _():
        ubuf[0:SUBLANES, :] = jnp.zeros((SUBLANES, D_CONV), f32)
        kbuf[:, 0:BLOCK, :] = jnp.zeros((KV_VARIANTS, BLOCK, LANES), bf16)
        vtbuf[:, :, 0:BLOCK] = jnp.zeros((KV_VARIANTS, LANES, BLOCK), bf16)

    def mixed_dot(lhs, rhs_f32):
        return lax.dot_general(lhs, rhs_f32, (((1,), (0,)), ((), ())), preferred_element_type=f32)

    x = x_ref[...]
    xg = x * gin_ref[...]
    q_unscaled = mixed_dot(xg.astype(bf16), win_ref[:, OFF_Q:OFF_Q + D_ATTN])
    r_in = lax.rsqrt(jnp.mean(x * x, axis=-1, keepdims=True) + RMS_EPS)
    h = (xg * r_in).astype(bf16)

    def proj(off, width):
        return mixed_dot(h, win_ref[:, off:off + width])

    qbuf[...] = (q_unscaled * (r_in * (HEAD_DIM ** -0.5 * LOG2E))).astype(bf16)
    kv = proj(OFF_KV, 2 * D_KV)
    k = kv[:, :D_KV]
    lo = lax.broadcasted_iota(jnp.int32, (ts, LANES), 1) < HEAD_DIM
    k_swapped = pltpu.roll(k, HEAD_DIM, axis=1)
    zero = jnp.zeros_like(k)
    for i, kk in enumerate([jnp.where(lo, k, zero), jnp.where(lo, zero, k_swapped),
                            jnp.where(lo, k_swapped, zero), jnp.where(lo, zero, k)]):
        kbuf[i, BLOCK:BLOCK + ts, :] = kk.astype(bf16)
    vt = kv[:, D_KV:].T.astype(bf16)
    vt0, vt1 = vt[:HEAD_DIM, :], vt[HEAD_DIM:, :]
    zrows = jnp.zeros_like(vt0)
    for i, vv in enumerate([(vt0, zrows), (zrows, vt0), (vt1, zrows), (zrows, vt1)]):
        vtbuf[i, :, BLOCK:BLOCK + ts] = jnp.concatenate(vv, axis=0)

    HALF = BLOCK // 2
    half_lanes = PAIRS_PER_KV * HALF
    kk = lax.broadcasted_iota(jnp.int32, (HALF, half_lanes), 0)
    qq = lax.broadcasted_iota(jnp.int32, (HALF, half_lanes), 1) & (HALF - 1)
    tri = kk <= qq
    lane_in_half = lax.broadcasted_iota(jnp.int32, (1, half_lanes), 1)
    neg = jnp.where(j == 0, -jnp.inf, 0.0).astype(f32)

    chunks = [(qb, g) for qb in range(ts // BLOCK) for g in range(N_KV_HEADS)]

    def scores(chunk):
        qb, g = chunk
        r0, t0 = qb * BLOCK, g * PAIRS_PER_KV
        out = []
        for half in range(2):
            q0 = r0 + half * HALF
            qs = jnp.concatenate(
                [qbuf[q0:q0 + HALF, (t0 + jp) * LANES:(t0 + jp + 1) * LANES] for jp in range(PAIRS_PER_KV)],
                axis=0)
            k0 = r0 + half * HALF
            kcat = jnp.concatenate([kbuf[2 * g, k0:k0 + 3 * HALF, :],
                                    kbuf[2 * g + 1, k0:k0 + 3 * HALF, :]], axis=0)
            out.append(lax.dot_general(kcat, qs, _NT, preferred_element_type=f32))
        return out

    def softmax(chunk, s_halves):
        qb, g = chunk
        p_cols, inv_sums = [], [[None, None], [None, None]]
        for half, s_all in enumerate(s_halves):
            p_rows = []
            for par in range(2):
                base = par * 3 * HALF
                if half == 0:
                    prev_lo, prev_hi = s_all[base:base + HALF], s_all[base + HALF:base + 2 * HALF]
                    cur_lo = s_all[base + 2 * HALF:base + 3 * HALF]
                    if qb == 0:
                        prev_lo, prev_hi = prev_lo + neg, prev_hi + neg
                    s = jnp.concatenate([jnp.where(tri, cur_lo, prev_lo), prev_hi], axis=0)
                else:
                    prev_hi = s_all[base:base + HALF]
                    cur_lo, cur_hi = s_all[base + HALF:base + 2 * HALF], s_all[base + 2 * HALF:base + 3 * HALF]
                    if qb == 0:
                        prev_hi = prev_hi + neg
                    s = jnp.concatenate([cur_lo, jnp.where(tri, cur_hi, prev_hi)], axis=0)
                s = s + bias_ref[g, par, half]
                sink = jnp.full((1, half_lanes), sinks_ref[0, g * Q_PER_KV + par] * LOG2E, f32)
                for jp in range(1, PAIRS_PER_KV):
                    sink = jnp.where(lane_in_half >= jp * HALF, sinks_ref[0, g * Q_PER_KV + 2 * jp + par] * LOG2E, sink)
                m = jnp.maximum(jnp.max(s, axis=0, keepdims=True), sink)
                p = jnp.exp2(s - m)
                inv_sums[par][half] = 1.0 / (jnp.sum(p, axis=0, keepdims=True) + jnp.exp2(sink - m))
                p_lo, p_hi = p[:HALF], p[HALF:]
                zero_p = jnp.zeros_like(p_lo)
                if half == 0:
                    p_prev = [jnp.where(tri, zero_p, p_lo), p_hi]
                    p_cur = [jnp.where(tri, p_lo, zero_p), zero_p]
                else:
                    p_prev = [zero_p, jnp.where(tri, zero_p, p_hi)]
                    p_cur = [p_lo, jnp.where(tri, p_hi, zero_p)]
                p_rows += p_prev + p_cur
            p_cols.append(jnp.concatenate(p_rows, axis=0).astype(bf16))
        inv = [jnp.concatenate(inv_sums[par], axis=1) for par in range(2)]
        return jnp.concatenate(p_cols, axis=1), inv

    def weighted_values(chunk, p_all, inv_sums):
        qb, g = chunk
        r0, t0 = qb * BLOCK, g * PAIRS_PER_KV
        vtcat = jnp.concatenate([vtbuf[2 * g, :, r0:r0 + 2 * BLOCK],
                                 vtbuf[2 * g + 1, :, r0:r0 + 2 * BLOCK]], axis=1)
        o_all = jnp.dot(vtcat, p_all, preferred_element_type=f32)
        scale = jnp.concatenate([jnp.broadcast_to(r, (HEAD_DIM, r.shape[1])) for r in inv_sums], axis=0)
        o_all = o_all * scale
        for half in range(2):
            o_half = o_all[:, half * half_lanes:(half + 1) * half_lanes].T
            for jp in range(PAIRS_PER_KV):
                attn_buf[r0 + half * HALF:r0 + (half + 1) * HALF, (t0 + jp) * LANES:(t0 + jp + 1) * LANES] = \
                    o_half[jp * HALF:(jp + 1) * HALF, :]

    def conv_elementwise(c, cc, cu, cb, gc):
        cs = slice(c * CONV_CHUNK, (c + 1) * CONV_CHUNK)
        u = cc * cu
        ubuf[SUBLANES:SUBLANES + ts, cs] = u
        u1 = ubuf[SUBLANES - 1:SUBLANES - 1 + ts, cs]
        u2 = ubuf[SUBLANES - 2:SUBLANES - 2 + ts, cs]
        ubuf[0:SUBLANES, cs] = ubuf[ts:ts + SUBLANES, cs]
        y = cb * (convw_ref[2, :, cs] * u + convw_ref[1, :, cs] * u1 + convw_ref[0, :, cs] * u2)
        cbuf[:, cs] = y * gconv_ref[:, cs] * _silu(gc)
        return jnp.sum(y * y, axis=-1, keepdims=True)

    n_conv = D_CONV // CONV_CHUNK
    assert len(chunks) == 2 * n_conv
    sgabuf[...] = _silu(proj(OFF_GA, D_ATTN))
    s_ahead = [scores(chunks[0]), scores(chunks[1])]
    ssq_c = jnp.zeros((ts, 1), f32)
    previous = None
    for c in range(n_conv):
        projected = []
        for i, off in ((2 * c, OFF_CC), (2 * c + 1, OFF_CU)):
            projected.append(proj(off + c * CONV_CHUNK, CONV_CHUNK))
            weighted_values(chunks[i], *softmax(chunks[i], s_ahead.pop(0)))
            if i + 2 < len(chunks):
                s_ahead.append(scores(chunks[i + 2]))
        projected.append(proj(OFF_CB + c * CONV_CHUNK, CONV_CHUNK))
        if previous is not None:
            ssq_c = ssq_c + conv_elementwise(c - 1, *previous)
        projected.append(proj(OFF_GC + c * CONV_CHUNK, CONV_CHUNK))
        previous = projected

    kbuf[:, 0:BLOCK, :] = kbuf[:, ts:ts + BLOCK, :]
    vtbuf[:, :, 0:BLOCK] = vtbuf[:, :, ts:ts + BLOCK]

    attn = attn_buf[...]
    r_a = lax.rsqrt(jnp.mean(attn * attn, axis=-1, keepdims=True) + RMS_EPS)
    attn_y = attn * r_a * gattn_ref[...] * sgabuf[...]
    o = mixed_dot(attn_y, wout_ref[D_CONV:D_MIX, :])
    ssq_c = ssq_c + conv_elementwise(n_conv - 1, *previous)
    r_c = lax.rsqrt(ssq_c * (1.0 / D_CONV) + RMS_EPS)
    o = o + mixed_dot(cbuf[...] * r_c, wout_ref[0:D_CONV, :])
    o_ref[...] = _rms(x + o, gfin_ref[...])


def _distance_bias():
    half = BLOCK // 2
    slopes = np.exp2(-8.0 * np.arange(1, N_Q_HEADS + 1, dtype=np.float32) / N_Q_HEADS).astype(np.float32)
    key = np.arange(BLOCK)[:, None]
    query = np.arange(BLOCK)[None, :]
    dist = np.where(key <= query, query - key, BLOCK + query - key).astype(np.float32)
    bias = (-slopes[:, None, None] * dist[None]).astype(np.float32) * np.float32(LOG2E)
    bias = bias.reshape(N_KV_HEADS, PAIRS_PER_KV, 2, BLOCK, 2, half)
    return bias.transpose(0, 2, 4, 3, 1, 5).reshape(N_KV_HEADS, 2, 2, BLOCK, PAIRS_PER_KV * half)


@jax.jit
def kernel(x, norm_in, w_in, conv_w, attn_sinks, norm_conv_out, norm_attn_out, w_out, norm_final):
    bsz, seq, d = x.shape
    assert d == D_MODEL and seq % SEQ_TILE == 0 and SEQ_TILE % BLOCK == 0
    assert w_in.shape == (1, D_MODEL, D_IN_PROJ) and w_out.shape == (1, D_MIX, D_MODEL)
    ts = SEQ_TILE
    const = lambda shape: pl.BlockSpec(shape, lambda b, j: (0,) * len(shape),
                                       pipeline_mode=pl.Buffered(1))
    return pl.pallas_call(
        _layer_kernel,
        grid=(bsz, seq // ts),
        in_specs=[
            pl.BlockSpec((None, ts, D_MODEL), lambda b, j: (b, j, 0)),
            const((1, D_MODEL)),
            const((D_MODEL, D_IN_PROJ)),
            const((CONV_WIDTH, 1, D_CONV)),
            pl.BlockSpec(memory_space=pltpu.SMEM),
            const((1, D_CONV)),
            const((1, D_ATTN)),
            const((D_MIX, D_MODEL)),
            const((1, D_MODEL)),
            const((N_KV_HEADS, 2, 2, BLOCK, PAIRS_PER_KV * BLOCK // 2)),
        ],
        out_specs=pl.BlockSpec((None, ts, D_MODEL), lambda b, j: (b, j, 0)),
        out_shape=jax.ShapeDtypeStruct(x.shape, x.dtype),
        scratch_shapes=[
            pltpu.VMEM((SUBLANES + ts, D_CONV), jnp.float32),
            pltpu.VMEM((KV_VARIANTS, BLOCK + ts, LANES), jnp.bfloat16),
            pltpu.VMEM((KV_VARIANTS, LANES, BLOCK + ts), jnp.bfloat16),
            pltpu.VMEM((ts, D_ATTN), jnp.bfloat16),
            pltpu.VMEM((ts, D_ATTN), jnp.float32),
            pltpu.VMEM((ts, D_CONV), jnp.float32),
            pltpu.VMEM((ts, D_ATTN), jnp.float32),
        ],
        compiler_params=pltpu.CompilerParams(
            dimension_semantics=("arbitrary", "arbitrary"),
            vmem_limit_bytes=VMEM_LIMIT_BYTES),
        name="hybrid_layer",
    )(x, norm_in, w_in[0], conv_w.reshape(CONV_WIDTH, 1, D_CONV), attn_sinks, norm_conv_out, norm_attn_out,
      w_out[0], norm_final[None, :], _distance_bias())
```

```python
import functools

import jax
import jax.numpy as jnp
import numpy as np
from jax import lax
from jax.experimental import pallas as pl
from jax.experimental.pallas import tpu as pltpu

D_MODEL = 1024
D_CONV = 1024
CONV_WIDTH = 3
N_Q_HEADS = 16
N_KV_HEADS = 2
HEAD_DIM = 64
Q_PER_KV = N_Q_HEADS // N_KV_HEADS
D_ATTN = N_Q_HEADS * HEAD_DIM
D_KV = N_KV_HEADS * HEAD_DIM
BLOCK = 128
D_MIX = D_CONV + D_ATTN
D_IN_PROJ = 4 * D_CONV + 2 * D_ATTN + 2 * D_KV
RMS_EPS = 1e-5
LOG2E = 1.4426950408889634

OFF_CB, OFF_CC, OFF_CU, OFF_GC = 0, D_CONV, 2 * D_CONV, 3 * D_CONV
OFF_Q = 4 * D_CONV
OFF_KV = OFF_Q + D_ATTN
OFF_GA = OFF_KV + 2 * D_KV

LANES = 128
SUBLANES = 8
PAIRS_PER_KV = Q_PER_KV * HEAD_DIM // LANES
KV_VARIANTS = 2 * N_KV_HEADS
SEQ_TILE = 512
CONV_CHUNK = 256
VMEM_LIMIT_BYTES = 60 * 1024 * 1024

_NT = (((1,), (1,)), ((), ()))


def _rms(x, gain):
    return x * lax.rsqrt(jnp.mean(x * x, axis=-1, keepdims=True) + RMS_EPS) * gain


def _silu(x):
    return x * (1.0 / (1.0 + jnp.exp2(x * -LOG2E)))


def _after(value, anchor):
    return jnp.where(anchor > jnp.inf, jnp.zeros_like(value), value)


def _layer_kernel(x_ref, gin_ref, win_ref, convw_ref, sinks_ref, gconv_ref, gattn_ref,
                  wout_ref, gfin_ref, bias_ref, o_ref, xrbuf, ubuf, kbuf, vtbuf, qbuf, attn_buf, cbuf, sgabuf,
                  *, n_tiles, tiles_per_seq):
    t = pl.program_id(0)

    @pl.when(t == 0)
    def _():
        xrbuf[...] = jnp.zeros(xrbuf.shape, jnp.float32)

    @pl.when(t < n_tiles)
    def _():
        _tile_body(t % tiles_per_seq, x_ref, gin_ref, win_ref, convw_ref, sinks_ref, gconv_ref, gattn_ref,
                   wout_ref, gfin_ref, bias_ref, o_ref, xrbuf, ubuf, kbuf, vtbuf, qbuf, attn_buf, cbuf, sgabuf)

    @pl.when(t == n_tiles)
    def _():
        o_ref[...] = _rms(xrbuf[...], gfin_ref[...])


def _tile_body(j, x_ref, gin_ref, win_ref, convw_ref, sinks_ref, gconv_ref, gattn_ref,
               wout_ref, gfin_ref, bias_ref, o_ref, xrbuf, ubuf, kbuf, vtbuf, qbuf, attn_buf, cbuf, sgabuf):
    ts = x_ref.shape[0]
    f32, bf16 = jnp.float32, jnp.bfloat16

    @pl.when(j == 0)
    def _():
        ubuf[0:SUBLANES, :] = jnp.zeros((SUBLANES, D_CONV), f32)
        kbuf[:, 0:BLOCK, :] = jnp.zeros((KV_VARIANTS, BLOCK, LANES), bf16)
        vtbuf[:, :, 0:BLOCK] = jnp.zeros((KV_VARIANTS, LANES, BLOCK), bf16)

    def mixed_dot(lhs, rhs_f32):
        return lax.dot_general(lhs, rhs_f32, (((1,), (0,)), ((), ())), preferred_element_type=f32)

    x = x_ref[...]
    xg = x * gin_ref[...]
    q_unscaled = mixed_dot(xg.astype(bf16), win_ref[:, OFF_Q:OFF_Q + D_ATTN])
    r_in = lax.rsqrt(jnp.mean(x * x, axis=-1, keepdims=True) + RMS_EPS)
    h = (xg * r_in).astype(bf16)

    def proj(off, width):
        return mixed_dot(h, win_ref[:, off:off + width])

    out_prev = _rms(xrbuf[...], gfin_ref[...])
    o_ref[...] = out_prev
    qbuf[...] = _after(q_unscaled * (r_in * (HEAD_DIM ** -0.5 * LOG2E)), out_prev).astype(bf16)
    kv = proj(OFF_KV, 2 * D_KV)
    k = kv[:, :D_KV]
    lo = lax.broadcasted_iota(jnp.int32, (ts, LANES), 1) < HEAD_DIM
    k_swapped = pltpu.roll(k, HEAD_DIM, axis=1)
    zero = jnp.zeros_like(k)
    for i, kk in enumerate([jnp.where(lo, k, zero), jnp.where(lo, zero, k_swapped),
                            jnp.where(lo, k_swapped, zero), jnp.where(lo, zero, k)]):
        kbuf[i, BLOCK:BLOCK + ts, :] = kk.astype(bf16)
    vt = kv[:, D_KV:].T.astype(bf16)
    vt0, vt1 = vt[:HEAD_DIM, :], vt[HEAD_DIM:, :]
    zrows = jnp.zeros_like(vt0)
    for i, vv in enumerate([(vt0, zrows), (zrows, vt0), (vt1, zrows), (zrows, vt1)]):
        vtbuf[i, :, BLOCK:BLOCK + ts] = jnp.concatenate(vv, axis=0)

    HALF = BLOCK // 2
    half_lanes = PAIRS_PER_KV * HALF
    kk = lax.broadcasted_iota(jnp.int32, (HALF, half_lanes), 0)
    qq = lax.broadcasted_iota(jnp.int32, (HALF, half_lanes), 1) & (HALF - 1)
    tri = kk <= qq
    lane_in_half = lax.broadcasted_iota(jnp.int32, (1, half_lanes), 1)
    neg = jnp.where(j == 0, -jnp.inf, 0.0).astype(f32)

    chunks = [(qb, g) for qb in range(ts // BLOCK) for g in range(N_KV_HEADS)]

    def scores(chunk):
        qb, g = chunk
        r0, t0 = qb * BLOCK, g * PAIRS_PER_KV
        out = []
        for half in range(2):
            q0 = r0 + half * HALF
            qs = jnp.concatenate(
                [qbuf[q0:q0 + HALF, (t0 + jp) * LANES:(t0 + jp + 1) * LANES] for jp in range(PAIRS_PER_KV)],
                axis=0)
            k0 = r0 + half * HALF
            kcat = jnp.concatenate([kbuf[2 * g, k0:k0 + 3 * HALF, :],
                                    kbuf[2 * g + 1, k0:k0 + 3 * HALF, :]], axis=0)
            out.append(lax.dot_general(kcat, qs, _NT, preferred_element_type=f32))
        return out

    def softmax(chunk, s_halves):
        qb, g = chunk
        p_cols, inv_sums = [], [[None, None], [None, None]]
        for half, s_all in enumerate(s_halves):
            p_rows = []
            for par in range(2):
                base = par * 3 * HALF
                if half == 0:
                    prev_lo, prev_hi = s_all[base:base + HALF], s_all[base + HALF:base + 2 * HALF]
                    cur_lo = s_all[base + 2 * HALF:base + 3 * HALF]
                    if qb == 0:
                        prev_lo, prev_hi = prev_lo + neg, prev_hi + neg
                    s = jnp.concatenate([jnp.where(tri, cur_lo, prev_lo), prev_hi], axis=0)
                else:
                    prev_hi = s_all[base:base + HALF]
                    cur_lo, cur_hi = s_all[base + HALF:base + 2 * HALF], s_all[base + 2 * HALF:base + 3 * HALF]
                    if qb == 0:
                        prev_hi = prev_hi + neg
                    s = jnp.concatenate([cur_lo, jnp.where(tri, cur_hi, prev_hi)], axis=0)
                s = s + bias_ref[g, par, half]
                sink = jnp.full((1, half_lanes), sinks_ref[0, g * Q_PER_KV + par] * LOG2E, f32)
                for jp in range(1, PAIRS_PER_KV):
                    sink = jnp.where(lane_in_half >= jp * HALF, sinks_ref[0, g * Q_PER_KV + 2 * jp + par] * LOG2E, sink)
                m = jnp.maximum(jnp.max(s, axis=0, keepdims=True), sink)
                p = jnp.exp2(s - m)
                inv_sums[par][half] = 1.0 / (jnp.sum(p, axis=0, keepdims=True) + jnp.exp2(sink - m))
                p_lo, p_hi = p[:HALF], p[HALF:]
                zero_p = jnp.zeros_like(p_lo)
                if half == 0:
                    p_prev = [jnp.where(tri, zero_p, p_lo), p_hi]
                    p_cur = [jnp.where(tri, p_lo, zero_p), zero_p]
                else:
                    p_prev = [zero_p, jnp.where(tri, zero_p, p_hi)]
                    p_cur = [p_lo, jnp.where(tri, p_hi, zero_p)]
                p_rows += p_prev + p_cur
            p_cols.append(jnp.concatenate(p_rows, axis=0).astype(bf16))
        inv = [jnp.concatenate(inv_sums[par], axis=1) for par in range(2)]
        return jnp.concatenate(p_cols, axis=1), inv

    def weighted_values(chunk, p_all, inv_sums):
        qb, g = chunk
        r0, t0 = qb * BLOCK, g * PAIRS_PER_KV
        vtcat = jnp.concatenate([vtbuf[2 * g, :, r0:r0 + 2 * BLOCK],
                                 vtbuf[2 * g + 1, :, r0:r0 + 2 * BLOCK]], axis=1)
        o_all = jnp.dot(vtcat, p_all, preferred_element_type=f32)
        scale = jnp.concatenate([jnp.broadcast_to(r, (HEAD_DIM, r.shape[1])) for r in inv_sums], axis=0)
        o_all = o_all * scale
        for half in range(2):
            o_half = o_all[:, half * half_lanes:(half + 1) * half_lanes].T
            for jp in range(PAIRS_PER_KV):
                attn_buf[r0 + half * HALF:r0 + (half + 1) * HALF, (t0 + jp) * LANES:(t0 + jp + 1) * LANES] = \
                    o_half[jp * HALF:(jp + 1) * HALF, :]

    def conv_elementwise(c, cc, cu, cb, gc):
        cs = slice(c * CONV_CHUNK, (c + 1) * CONV_CHUNK)
        u = cc * cu
        ubuf[SUBLANES:SUBLANES + ts, cs] = u
        u1 = ubuf[SUBLANES - 1:SUBLANES - 1 + ts, cs]
        u2 = ubuf[SUBLANES - 2:SUBLANES - 2 + ts, cs]
        ubuf[0:SUBLANES, cs] = ubuf[ts:ts + SUBLANES, cs]
        y = cb * (convw_ref[2, :, cs] * u + convw_ref[1, :, cs] * u1 + convw_ref[0, :, cs] * u2)
        cbuf[:, cs] = y * gconv_ref[:, cs] * _silu(gc)
        return jnp.sum(y * y, axis=-1, keepdims=True)

    n_conv = D_CONV // CONV_CHUNK
    assert len(chunks) == 2 * n_conv
    sgabuf[...] = _silu(proj(OFF_GA, D_ATTN))
    s_ahead = [scores(chunks[0]), scores(chunks[1])]
    ssq_c = jnp.zeros((ts, 1), f32)
    previous = None
    for c in range(n_conv):
        projected = []
        for i, off in ((2 * c, OFF_CC), (2 * c + 1, OFF_CU)):
            projected.append(proj(off + c * CONV_CHUNK, CONV_CHUNK))
            weighted_values(chunks[i], *softmax(chunks[i], s_ahead.pop(0)))
            if i + 2 < len(chunks):
                s_ahead.append(scores(chunks[i + 2]))
        projected.append(proj(OFF_CB + c * CONV_CHUNK, CONV_CHUNK))
        if previous is not None:
            ssq_c = ssq_c + conv_elementwise(c - 1, *previous)
        projected.append(proj(OFF_GC + c * CONV_CHUNK, CONV_CHUNK))
        previous = projected

    kbuf[:, 0:BLOCK, :] = kbuf[:, ts:ts + BLOCK, :]
    vtbuf[:, :, 0:BLOCK] = vtbuf[:, :, ts:ts + BLOCK]

    attn = attn_buf[...]
    attn_y = attn * gattn_ref[...] * sgabuf[...]
    split = (n_conv - 1) * CONV_CHUNK
    o_c = mixed_dot(cbuf[:, 0:split], wout_ref[0:split, :])
    ssq_c = ssq_c + conv_elementwise(n_conv - 1, *previous)
    o_a = mixed_dot(attn_y, wout_ref[D_CONV:D_MIX, :])
    o_c = o_c + mixed_dot(cbuf[:, split:D_CONV], wout_ref[split:D_CONV, :])
    r_a = lax.rsqrt(jnp.mean(attn * attn, axis=-1, keepdims=True) + RMS_EPS)
    r_c = lax.rsqrt(ssq_c * (1.0 / D_CONV) + RMS_EPS)
    xrbuf[...] = x + r_a * o_a + r_c * o_c


def _distance_bias():
    half = BLOCK // 2
    slopes = np.exp2(-8.0 * np.arange(1, N_Q_HEADS + 1, dtype=np.float32) / N_Q_HEADS).astype(np.float32)
    key = np.arange(BLOCK)[:, None]
    query = np.arange(BLOCK)[None, :]
    dist = np.where(key <= query, query - key, BLOCK + query - key).astype(np.float32)
    bias = (-slopes[:, None, None] * dist[None]).astype(np.float32) * np.float32(LOG2E)
    bias = bias.reshape(N_KV_HEADS, PAIRS_PER_KV, 2, BLOCK, 2, half)
    return bias.transpose(0, 2, 4, 3, 1, 5).reshape(N_KV_HEADS, 2, 2, BLOCK, PAIRS_PER_KV * half)


@jax.jit
def kernel(x, norm_in, w_in, conv_w, attn_sinks, norm_conv_out, norm_attn_out, w_out, norm_final):
    bsz, seq, d = x.shape
    assert d == D_MODEL and seq % SEQ_TILE == 0 and SEQ_TILE % BLOCK == 0
    assert w_in.shape == (1, D_MODEL, D_IN_PROJ) and w_out.shape == (1, D_MIX, D_MODEL)
    ts = SEQ_TILE
    tiles_per_seq = seq // ts
    n_tiles = bsz * tiles_per_seq

    def x_tile(shift):
        def index_map(t):
            i = jnp.clip(t + shift, 0, n_tiles - 1)
            return (i // tiles_per_seq, i % tiles_per_seq, 0)
        return pl.BlockSpec((None, ts, D_MODEL), index_map)

    const = lambda shape: pl.BlockSpec(shape, lambda t: (0,) * len(shape), pipeline_mode=pl.Buffered(1))
    return pl.pallas_call(
        functools.partial(_layer_kernel, n_tiles=n_tiles, tiles_per_seq=tiles_per_seq),
        grid=(n_tiles + 1,),
        in_specs=[
            x_tile(0),
            const((1, D_MODEL)),
            const((D_MODEL, D_IN_PROJ)),
            const((CONV_WIDTH, 1, D_CONV)),
            pl.BlockSpec(memory_space=pltpu.SMEM),
            const((1, D_CONV)),
            const((1, D_ATTN)),
            const((D_MIX, D_MODEL)),
            const((1, D_MODEL)),
            const((N_KV_HEADS, 2, 2, BLOCK, PAIRS_PER_KV * BLOCK // 2)),
        ],
        out_specs=x_tile(-1),
        out_shape=jax.ShapeDtypeStruct(x.shape, x.dtype),
        scratch_shapes=[
            pltpu.VMEM((ts, D_MODEL), jnp.float32),
            pltpu.VMEM((SUBLANES + ts, D_CONV), jnp.float32),
            pltpu.VMEM((KV_VARIANTS, BLOCK + ts, LANES), jnp.bfloat16),
            pltpu.VMEM((KV_VARIANTS, LANES, BLOCK + ts), jnp.bfloat16),
            pltpu.VMEM((ts, D_ATTN), jnp.bfloat16),
            pltpu.VMEM((ts, D_ATTN), jnp.float32),
            pltpu.VMEM((ts, D_CONV), jnp.float32),
            pltpu.VMEM((ts, D_ATTN), jnp.float32),
        ],
        compiler_params=pltpu.CompilerParams(
            dimension_semantics=("arbitrary",),
            vmem_limit_bytes=VMEM_LIMIT_BYTES),
        name="hybrid_layer",
    )(x, norm_in, w_in[0], conv_w.reshape(CONV_WIDTH, 1, D_CONV), attn_sinks, norm_conv_out, norm_attn_out,
      w_out[0], norm_final[None, :], _distance_bias())
```

```python
import jax
import jax.numpy as jnp
import numpy as np
from jax import lax
from jax.experimental import pallas as pl
from jax.experimental.pallas import tpu as pltpu

D_MODEL = 1024
D_CONV = 1024
CONV_WIDTH = 3
N_Q_HEADS = 16
N_KV_HEADS = 2
HEAD_DIM = 64
Q_PER_KV = N_Q_HEADS // N_KV_HEADS
D_ATTN = N_Q_HEADS * HEAD_DIM
D_KV = N_KV_HEADS * HEAD_DIM
BLOCK = 128
D_MIX = D_CONV + D_ATTN
D_IN_PROJ = 4 * D_CONV + 2 * D_ATTN + 2 * D_KV
RMS_EPS = 1e-5
LOG2E = 1.4426950408889634

OFF_CB, OFF_CC, OFF_CU, OFF_GC = 0, D_CONV, 2 * D_CONV, 3 * D_CONV
OFF_Q = 4 * D_CONV
OFF_KV = OFF_Q + D_ATTN
OFF_GA = OFF_KV + 2 * D_KV

LANES = 128
SUBLANES = 8
PAIRS_PER_KV = Q_PER_KV * HEAD_DIM // LANES
KV_VARIANTS = 2 * N_KV_HEADS
SEQ_TILE = 512
CONV_CHUNK = 256
VMEM_LIMIT_BYTES = 60 * 1024 * 1024

_NT = (((1,), (1,)), ((), ()))


def _rms(x, gain):
    return x * lax.rsqrt(jnp.mean(x * x, axis=-1, keepdims=True) + RMS_EPS) * gain


def _silu(x):
    return x * (1.0 / (1.0 + jnp.exp2(x * -LOG2E)))


def _layer_kernel(x_ref, gin_ref, win_ref, convw_ref, sinks_ref, gconv_ref, gattn_ref,
                  wout_ref, gfin_ref, bias_ref, o_ref, ubuf, kbuf, vtbuf, qbuf, attn_buf, cbuf, sgabuf):
    ts = x_ref.shape[0]
    j = pl.program_id(1)
    f32, bf16 = jnp.float32, jnp.bfloat16

    @pl.when(j == 0)
    def _():
        ubuf[0:SUBLANES, :] = jnp.zeros((SUBLANES, D_CONV), f32)
        kbuf[:, 0:BLOCK, :] = jnp.zeros((KV_VARIANTS, BLOCK, LANES), bf16)
        vtbuf[:, :, 0:BLOCK] = jnp.zeros((N_KV_HEADS, HEAD_DIM, BLOCK), bf16)

    def mixed_dot(lhs, rhs_f32):
        return lax.dot_general(lhs, rhs_f32, (((1,), (0,)), ((), ())), preferred_element_type=f32)

    x = x_ref[...]
    xg = x * gin_ref[...]
    q_unscaled = mixed_dot(xg.astype(bf16), win_ref[:, OFF_Q:OFF_Q + D_ATTN])
    r_in = lax.rsqrt(jnp.mean(x * x, axis=-1, keepdims=True) + RMS_EPS)
    h = (xg * r_in).astype(bf16)

    def proj(off, width):
        return mixed_dot(h, win_ref[:, off:off + width])

    qbuf[...] = (q_unscaled * (r_in * (HEAD_DIM ** -0.5 * LOG2E))).astype(bf16)
    kv = proj(OFF_KV, 2 * D_KV)
    k = kv[:, :D_KV]
    lo = lax.broadcasted_iota(jnp.int32, (ts, LANES), 1) < HEAD_DIM
    k_swapped = pltpu.roll(k, HEAD_DIM, axis=1)
    zero = jnp.zeros_like(k)
    for i, kk in enumerate([jnp.where(lo, k, zero), jnp.where(lo, zero, k_swapped),
                            jnp.where(lo, k_swapped, zero), jnp.where(lo, zero, k)]):
        kbuf[i, BLOCK:BLOCK + ts, :] = kk.astype(bf16)
    vt = kv[:, D_KV:].T.astype(bf16)
    for g in range(N_KV_HEADS):
        vtbuf[g, :, BLOCK:BLOCK + ts] = vt[g * HEAD_DIM:(g + 1) * HEAD_DIM, :]

    HALF = BLOCK // 2
    half_lanes = PAIRS_PER_KV * HALF
    kk = lax.broadcasted_iota(jnp.int32, (HALF, half_lanes), 0)
    qq = lax.broadcasted_iota(jnp.int32, (HALF, half_lanes), 1) & (HALF - 1)
    tri = kk <= qq
    lane_in_half = lax.broadcasted_iota(jnp.int32, (1, half_lanes), 1)
    neg = jnp.where(j == 0, -jnp.inf, 0.0).astype(f32)

    chunks = [(qb, g) for qb in range(ts // BLOCK) for g in range(N_KV_HEADS)]

    def scores(chunk):
        qb, g = chunk
        r0, t0 = qb * BLOCK, g * PAIRS_PER_KV
        out = []
        for half in range(2):
            q0 = r0 + half * HALF
            qs = jnp.concatenate(
                [qbuf[q0:q0 + HALF, (t0 + jp) * LANES:(t0 + jp + 1) * LANES] for jp in range(PAIRS_PER_KV)],
                axis=0)
            k0 = r0 + half * HALF
            kcat = jnp.concatenate([kbuf[2 * g, k0:k0 + 3 * HALF, :],
                                    kbuf[2 * g + 1, k0:k0 + 3 * HALF, :]], axis=0)
            out.append(lax.dot_general(kcat, qs, _NT, preferred_element_type=f32))
        return out

    def softmax(chunk, s_halves):
        qb, g = chunk
        p_blocks, inv_sums = [[None, None], [None, None]], [[None, None], [None, None]]
        for half, s_all in enumerate(s_halves):
            for par in range(2):
                base = par * 3 * HALF
                if half == 0:
                    prev_lo, prev_hi = s_all[base:base + HALF], s_all[base + HALF:base + 2 * HALF]
                    cur_lo = s_all[base + 2 * HALF:base + 3 * HALF]
                    if qb == 0:
                        prev_lo, prev_hi = prev_lo + neg, prev_hi + neg
                    s = jnp.concatenate([jnp.where(tri, cur_lo, prev_lo), prev_hi], axis=0)
                else:
                    prev_hi = s_all[base:base + HALF]
                    cur_lo, cur_hi = s_all[base + HALF:base + 2 * HALF], s_all[base + 2 * HALF:base + 3 * HALF]
                    if qb == 0:
                        prev_hi = prev_hi + neg
                    s = jnp.concatenate([cur_lo, jnp.where(tri, cur_hi, prev_hi)], axis=0)
                s = s + bias_ref[g, par, half]
                sink = jnp.full((1, half_lanes), sinks_ref[0, g * Q_PER_KV + par] * LOG2E, f32)
                for jp in range(1, PAIRS_PER_KV):
                    sink = jnp.where(lane_in_half >= jp * HALF, sinks_ref[0, g * Q_PER_KV + 2 * jp + par] * LOG2E, sink)
                m = jnp.maximum(jnp.max(s, axis=0, keepdims=True), sink)
                p = jnp.exp2(s - m)
                inv_sums[par][half] = 1.0 / (jnp.sum(p, axis=0, keepdims=True) + jnp.exp2(sink - m))
                p_lo, p_hi = p[:HALF], p[HALF:]
                zero_p = jnp.zeros_like(p_lo)
                if half == 0:
                    p_prev = [jnp.where(tri, zero_p, p_lo), p_hi]
                    p_cur = [jnp.where(tri, p_lo, zero_p), zero_p]
                else:
                    p_prev = [zero_p, jnp.where(tri, zero_p, p_hi)]
                    p_cur = [p_lo, jnp.where(tri, p_hi, zero_p)]
                p_blocks[par][half] = jnp.concatenate(p_prev + p_cur, axis=0).astype(bf16)
        order = [(par, half) for par in range(2) for half in range(2)]
        return (jnp.concatenate([p_blocks[par][half] for par, half in order], axis=1),
                jnp.concatenate([inv_sums[par][half] for par, half in order], axis=1))

    def weighted_values(chunk, p_all, inv_sums):
        qb, g = chunk
        r0, t0 = qb * BLOCK, g * PAIRS_PER_KV
        o_all = jnp.dot(vtbuf[g, :, r0:r0 + 2 * BLOCK], p_all, preferred_element_type=f32) * inv_sums
        for half in range(2):
            o_half = jnp.concatenate([o_all[:, (2 * par + half) * half_lanes:(2 * par + half + 1) * half_lanes]
                                      for par in range(2)], axis=0).T
            for jp in range(PAIRS_PER_KV):
                attn_buf[r0 + half * HALF:r0 + (half + 1) * HALF, (t0 + jp) * LANES:(t0 + jp + 1) * LANES] = \
                    o_half[jp * HALF:(jp + 1) * HALF, :]

    def conv_elementwise(c, cc, cu, cb, gc):
        cs = slice(c * CONV_CHUNK, (c + 1) * CONV_CHUNK)
        u = cc * cu
        ubuf[SUBLANES:SUBLANES + ts, cs] = u
        u1 = ubuf[SUBLANES - 1:SUBLANES - 1 + ts, cs]
        u2 = ubuf[SUBLANES - 2:SUBLANES - 2 + ts, cs]
        ubuf[0:SUBLANES, cs] = ubuf[ts:ts + SUBLANES, cs]
        y = cb * (convw_ref[2, :, cs] * u + convw_ref[1, :, cs] * u1 + convw_ref[0, :, cs] * u2)
        cbuf[:, cs] = y * gconv_ref[:, cs] * _silu(gc)
        return jnp.sum(y * y, axis=-1, keepdims=True)

    n_conv = D_CONV // CONV_CHUNK
    assert len(chunks) == 2 * n_conv
    sgabuf[...] = _silu(proj(OFF_GA, D_ATTN))
    s_ahead = [scores(chunks[0]), scores(chunks[1])]
    ssq_c = jnp.zeros((ts, 1), f32)
    previous = None
    for c in range(n_conv):
        projected = []
        for i, off in ((2 * c, OFF_CC), (2 * c + 1, OFF_CU)):
            projected.append(proj(off + c * CONV_CHUNK, CONV_CHUNK))
            weighted_values(chunks[i], *softmax(chunks[i], s_ahead.pop(0)))
            if i + 2 < len(chunks):
                s_ahead.append(scores(chunks[i + 2]))
        projected.append(proj(OFF_CB + c * CONV_CHUNK, CONV_CHUNK))
        if previous is not None:
            ssq_c = ssq_c + conv_elementwise(c - 1, *previous)
        projected.append(proj(OFF_GC + c * CONV_CHUNK, CONV_CHUNK))
        previous = projected

    kbuf[:, 0:BLOCK, :] = kbuf[:, ts:ts + BLOCK, :]
    vtbuf[:, :, 0:BLOCK] = vtbuf[:, :, ts:ts + BLOCK]

    attn = attn_buf[...]
    attn_y = attn * gattn_ref[...] * sgabuf[...]
    split = (n_conv - 1) * CONV_CHUNK
    o_c = mixed_dot(cbuf[:, 0:split], wout_ref[0:split, :])
    ssq_c = ssq_c + conv_elementwise(n_conv - 1, *previous)
    o_a = mixed_dot(attn_y, wout_ref[D_CONV:D_MIX, :])
    o_c = o_c + mixed_dot(cbuf[:, split:D_CONV], wout_ref[split:D_CONV, :])
    r_a = lax.rsqrt(jnp.mean(attn * attn, axis=-1, keepdims=True) + RMS_EPS)
    r_c = lax.rsqrt(ssq_c * (1.0 / D_CONV) + RMS_EPS)
    o_ref[...] = _rms(x + r_a * o_a + r_c * o_c, gfin_ref[...])


def _distance_bias():
    half = BLOCK // 2
    slopes = np.exp2(-8.0 * np.arange(1, N_Q_HEADS + 1, dtype=np.float32) / N_Q_HEADS).astype(np.float32)
    key = np.arange(BLOCK)[:, None]
    query = np.arange(BLOCK)[None, :]
    dist = np.where(key <= query, query - key, BLOCK + query - key).astype(np.float32)
    bias = (-slopes[:, None, None] * dist[None]).astype(np.float32) * np.float32(LOG2E)
    bias = bias.reshape(N_KV_HEADS, PAIRS_PER_KV, 2, BLOCK, 2, half)
    return bias.transpose(0, 2, 4, 3, 1, 5).reshape(N_KV_HEADS, 2, 2, BLOCK, PAIRS_PER_KV * half)


@jax.jit
def kernel(x, norm_in, w_in, conv_w, attn_sinks, norm_conv_out, norm_attn_out, w_out, norm_final):
    bsz, seq, d = x.shape
    assert d == D_MODEL and seq % SEQ_TILE == 0 and SEQ_TILE % BLOCK == 0
    assert w_in.shape == (1, D_MODEL, D_IN_PROJ) and w_out.shape == (1, D_MIX, D_MODEL)
    ts = SEQ_TILE
    const = lambda shape: pl.BlockSpec(shape, lambda b, j: (0,) * len(shape),
                                       pipeline_mode=pl.Buffered(1))
    return pl.pallas_call(
        _layer_kernel,
        grid=(bsz, seq // ts),
        in_specs=[
            pl.BlockSpec((None, ts, D_MODEL), lambda b, j: (b, j, 0)),
            const((1, D_MODEL)),
            const((D_MODEL, D_IN_PROJ)),
            const((CONV_WIDTH, 1, D_CONV)),
            pl.BlockSpec(memory_space=pltpu.SMEM),
            const((1, D_CONV)),
            const((1, D_ATTN)),
            const((D_MIX, D_MODEL)),
            const((1, D_MODEL)),
            const((N_KV_HEADS, 2, 2, BLOCK, PAIRS_PER_KV * BLOCK // 2)),
        ],
        out_specs=pl.BlockSpec((None, ts, D_MODEL), lambda b, j: (b, j, 0)),
        out_shape=jax.ShapeDtypeStruct(x.shape, x.dtype),
        scratch_shapes=[
            pltpu.VMEM((SUBLANES + ts, D_CONV), jnp.float32),
            pltpu.VMEM((KV_VARIANTS, BLOCK + ts, LANES), jnp.bfloat16),
            pltpu.VMEM((N_KV_HEADS, HEAD_DIM, BLOCK + ts), jnp.bfloat16),
            pltpu.VMEM((ts, D_ATTN), jnp.bfloat16),
            pltpu.VMEM((ts, D_ATTN), jnp.float32),
            pltpu.VMEM((ts, D_CONV), jnp.float32),
            pltpu.VMEM((ts, D_ATTN), jnp.float32),
        ],
        compiler_params=pltpu.CompilerParams(
            dimension_semantics=("arbitrary", "arbitrary"),
            vmem_limit_bytes=VMEM_LIMIT_BYTES),
        name="hybrid_layer",
    )(x, norm_in, w_in[0], conv_w.reshape(CONV_WIDTH, 1, D_CONV), attn_sinks, norm_conv_out, norm_attn_out,
      w_out[0], norm_final[None, :], _distance_bias())
```

```python
import jax
import jax.numpy as jnp
import numpy as np
from jax import lax
from jax.experimental import pallas as pl
from jax.experimental.pallas import tpu as pltpu

D_MODEL = 1024
D_CONV = 1024
CONV_WIDTH = 3
N_Q_HEADS = 16
N_KV_HEADS = 2
HEAD_DIM = 64
Q_PER_KV = N_Q_HEADS // N_KV_HEADS
D_ATTN = N_Q_HEADS * HEAD_DIM
D_KV = N_KV_HEADS * HEAD_DIM
BLOCK = 128
D_MIX = D_CONV + D_ATTN
D_IN_PROJ = 4 * D_CONV + 2 * D_ATTN + 2 * D_KV
RMS_EPS = 1e-5
LOG2E = 1.4426950408889634

OFF_CB, OFF_CC, OFF_CU, OFF_GC = 0, D_CONV, 2 * D_CONV, 3 * D_CONV
OFF_Q = 4 * D_CONV
OFF_KV = OFF_Q + D_ATTN
OFF_GA = OFF_KV + 2 * D_KV

LANES = 128
SUBLANES = 8
PAIRS_PER_KV = Q_PER_KV * HEAD_DIM // LANES
KV_VARIANTS = 2 * N_KV_HEADS
SEQ_TILE = 512
CONV_CHUNK = 256
VMEM_LIMIT_BYTES = 60 * 1024 * 1024

_NT = (((1,), (1,)), ((), ()))


def _rms(x, gain):
    return x * lax.rsqrt(jnp.mean(x * x, axis=-1, keepdims=True) + RMS_EPS) * gain


def _silu(x):
    return x * (1.0 / (1.0 + jnp.exp2(x * -LOG2E)))


def _layer_kernel(x_ref, gin_ref, win_ref, convw_ref, sinks_ref, gconv_ref, gattn_ref,
                  wout_ref, gfin_ref, bias_ref, o_ref, ubuf, kbuf, vtbuf, qbuf, attn_buf, cbuf, sgabuf):
    ts = x_ref.shape[0]
    j = pl.program_id(1)
    f32, bf16 = jnp.float32, jnp.bfloat16

    @pl.when(j == 0)
    def _():
        ubuf[0:SUBLANES, :] = jnp.zeros((SUBLANES, D_CONV), f32)
        kbuf[:, 0:BLOCK, :] = jnp.zeros((KV_VARIANTS, BLOCK, LANES), bf16)
        vtbuf[:, :, 0:BLOCK] = jnp.zeros((KV_VARIANTS, LANES, BLOCK), bf16)

    def mixed_dot(lhs, rhs_f32):
        return lax.dot_general(lhs, rhs_f32, (((1,), (0,)), ((), ())), preferred_element_type=f32)

    x = x_ref[...]
    xg = x * gin_ref[...]
    qkv_unscaled = mixed_dot(xg.astype(bf16), win_ref[:, OFF_Q:OFF_GA])
    q_unscaled = qkv_unscaled[:, :D_ATTN]
    r_in = lax.rsqrt(jnp.mean(x * x, axis=-1, keepdims=True) + RMS_EPS)
    h = (xg * r_in).astype(bf16)

    def proj(off, width):
        return mixed_dot(h, win_ref[:, off:off + width])

    qbuf[...] = (q_unscaled * (r_in * (HEAD_DIM ** -0.5 * LOG2E))).astype(bf16)
    kv = qkv_unscaled[:, D_ATTN:] * r_in
    k = kv[:, :D_KV]
    lo = lax.broadcasted_iota(jnp.int32, (ts, LANES), 1) < HEAD_DIM
    k_swapped = pltpu.roll(k, HEAD_DIM, axis=1)
    zero = jnp.zeros_like(k)
    for i, kk in enumerate([jnp.where(lo, k, zero), jnp.where(lo, zero, k_swapped),
                            jnp.where(lo, k_swapped, zero), jnp.where(lo, zero, k)]):
        kbuf[i, BLOCK:BLOCK + ts, :] = kk.astype(bf16)
    vt = kv[:, D_KV:].T.astype(bf16)
    vt0, vt1 = vt[:HEAD_DIM, :], vt[HEAD_DIM:, :]
    zrows = jnp.zeros_like(vt0)
    for i, vv in enumerate([(vt0, zrows), (zrows, vt0), (vt1, zrows), (zrows, vt1)]):
        vtbuf[i, :, BLOCK:BLOCK + ts] = jnp.concatenate(vv, axis=0)

    HALF = BLOCK // 2
    half_lanes = PAIRS_PER_KV * HALF
    kk = lax.broadcasted_iota(jnp.int32, (HALF, half_lanes), 0)
    qq = lax.broadcasted_iota(jnp.int32, (HALF, half_lanes), 1) & (HALF - 1)
    tri = kk <= qq
    lane_in_half = lax.broadcasted_iota(jnp.int32, (1, half_lanes), 1)
    neg = jnp.where(j == 0, -jnp.inf, 0.0).astype(f32)

    chunks = [(qb, g) for qb in range(ts // BLOCK) for g in range(N_KV_HEADS)]

    def scores(chunk):
        qb, g = chunk
        r0, t0 = qb * BLOCK, g * PAIRS_PER_KV
        out = []
        for half in range(2):
            q0 = r0 + half * HALF
            qs = jnp.concatenate(
                [qbuf[q0:q0 + HALF, (t0 + jp) * LANES:(t0 + jp + 1) * LANES] for jp in range(PAIRS_PER_KV)],
                axis=0)
            k0 = r0 + half * HALF
            kcat = jnp.concatenate([kbuf[2 * g, k0:k0 + 3 * HALF, :],
                                    kbuf[2 * g + 1, k0:k0 + 3 * HALF, :]], axis=0)
            out.append(lax.dot_general(kcat, qs, _NT, preferred_element_type=f32))
        return out

    def softmax(chunk, s_halves):
        qb, g = chunk
        p_cols, inv_sums = [], [[None, None], [None, None]]
        for half, s_all in enumerate(s_halves):
            p_rows = []
            for par in range(2):
                base = par * 3 * HALF
                if half == 0:
                    prev_lo, prev_hi = s_all[base:base + HALF], s_all[base + HALF:base + 2 * HALF]
                    cur_lo = s_all[base + 2 * HALF:base + 3 * HALF]
                    if qb == 0:
                        prev_lo, prev_hi = prev_lo + neg, prev_hi + neg
                    s = jnp.concatenate([jnp.where(tri, cur_lo, prev_lo), prev_hi], axis=0)
                else:
                    prev_hi = s_all[base:base + HALF]
                    cur_lo, cur_hi = s_all[base + HALF:base + 2 * HALF], s_all[base + 2 * HALF:base + 3 * HALF]
                    if qb == 0:
                        prev_hi = prev_hi + neg
                    s = jnp.concatenate([cur_lo, jnp.where(tri, cur_hi, prev_hi)], axis=0)
                s = s + bias_ref[g, par, half]
                sink = jnp.full((1, half_lanes), sinks_ref[0, g * Q_PER_KV + par] * LOG2E, f32)
                for jp in range(1, PAIRS_PER_KV):
                    sink = jnp.where(lane_in_half >= jp * HALF, sinks_ref[0, g * Q_PER_KV + 2 * jp + par] * LOG2E, sink)
                m = jnp.maximum(jnp.max(s, axis=0, keepdims=True), sink)
                p = jnp.exp2(s - m)
                inv_sums[par][half] = 1.0 / (jnp.sum(p, axis=0, keepdims=True) + jnp.exp2(sink - m))
                p_lo, p_hi = p[:HALF], p[HALF:]
                zero_p = jnp.zeros_like(p_lo)
                if half == 0:
                    p_prev = [jnp.where(tri, zero_p, p_lo), p_hi]
                    p_cur = [jnp.where(tri, p_lo, zero_p), zero_p]
                else:
                    p_prev = [zero_p, jnp.where(tri, zero_p, p_hi)]
                    p_cur = [p_lo, jnp.where(tri, p_hi, zero_p)]
                p_rows += p_prev + p_cur
            p_cols.append(jnp.concatenate(p_rows, axis=0).astype(bf16))
        inv = [jnp.concatenate(inv_sums[par], axis=1) for par in range(2)]
        return jnp.concatenate(p_cols, axis=1), inv

    def weighted_values(chunk, p_all, inv_sums):
        qb, g = chunk
        r0, t0 = qb * BLOCK, g * PAIRS_PER_KV
        vtcat = jnp.concatenate([vtbuf[2 * g, :, r0:r0 + 2 * BLOCK],
                                 vtbuf[2 * g + 1, :, r0:r0 + 2 * BLOCK]], axis=1)
        o_all = jnp.dot(vtcat, p_all, preferred_element_type=f32)
        scale = jnp.concatenate([jnp.broadcast_to(r, (HEAD_DIM, r.shape[1])) for r in inv_sums], axis=0)
        o_all = o_all * scale
        for half in range(2):
            o_half = o_all[:, half * half_lanes:(half + 1) * half_lanes].T
            for jp in range(PAIRS_PER_KV):
                attn_buf[r0 + half * HALF:r0 + (half + 1) * HALF, (t0 + jp) * LANES:(t0 + jp + 1) * LANES] = \
                    o_half[jp * HALF:(jp + 1) * HALF, :]

    def conv_elementwise(c, cc, cu, cb, gc):
        cs = slice(c * CONV_CHUNK, (c + 1) * CONV_CHUNK)
        u = cc * cu
        ubuf[SUBLANES:SUBLANES + ts, cs] = u
        u1 = ubuf[SUBLANES - 1:SUBLANES - 1 + ts, cs]
        u2 = ubuf[SUBLANES - 2:SUBLANES - 2 + ts, cs]
        ubuf[0:SUBLANES, cs] = ubuf[ts:ts + SUBLANES, cs]
        y = cb * (convw_ref[2, :, cs] * u + convw_ref[1, :, cs] * u1 + convw_ref[0, :, cs] * u2)
        cbuf[:, cs] = y * gconv_ref[:, cs] * _silu(gc)
        return jnp.sum(y * y, axis=-1, keepdims=True)

    n_conv = D_CONV // CONV_CHUNK
    assert len(chunks) == 2 * n_conv
    sgabuf[...] = _silu(proj(OFF_GA, D_ATTN))
    s_ahead = [scores(chunks[0]), scores(chunks[1])]
    ssq_c = jnp.zeros((ts, 1), f32)
    previous = None
    for c in range(n_conv):
        projected = []
        for i, off in ((2 * c, OFF_CC), (2 * c + 1, OFF_CU)):
            projected.append(proj(off + c * CONV_CHUNK, CONV_CHUNK))
            weighted_values(chunks[i], *softmax(chunks[i], s_ahead.pop(0)))
            if i + 2 < len(chunks):
                s_ahead.append(scores(chunks[i + 2]))
        projected.append(proj(OFF_CB + c * CONV_CHUNK, CONV_CHUNK))
        if previous is not None:
            ssq_c = ssq_c + conv_elementwise(c - 1, *previous)
        projected.append(proj(OFF_GC + c * CONV_CHUNK, CONV_CHUNK))
        previous = projected

    kbuf[:, 0:BLOCK, :] = kbuf[:, ts:ts + BLOCK, :]
    vtbuf[:, :, 0:BLOCK] = vtbuf[:, :, ts:ts + BLOCK]

    attn = attn_buf[...]
    attn_y = attn * gattn_ref[...] * sgabuf[...]
    split = (n_conv - 1) * CONV_CHUNK
    o_c = mixed_dot(cbuf[:, 0:split], wout_ref[0:split, :])
    ssq_c = ssq_c + conv_elementwise(n_conv - 1, *previous)
    o_a = mixed_dot(attn_y, wout_ref[D_CONV:D_MIX, :])
    o_c = o_c + mixed_dot(cbuf[:, split:D_CONV], wout_ref[split:D_CONV, :])
    r_a = lax.rsqrt(jnp.mean(attn * attn, axis=-1, keepdims=True) + RMS_EPS)
    r_c = lax.rsqrt(ssq_c * (1.0 / D_CONV) + RMS_EPS)
    o_ref[...] = _rms(x + r_a * o_a + r_c * o_c, gfin_ref[...])


def _distance_bias():
    half = BLOCK // 2
    slopes = np.exp2(-8.0 * np.arange(1, N_Q_HEADS + 1, dtype=np.float32) / N_Q_HEADS).astype(np.float32)
    key = np.arange(BLOCK)[:, None]
    query = np.arange(BLOCK)[None, :]
    dist = np.where(key <= query, query - key, BLOCK + query - key).astype(np.float32)
    bias = (-slopes[:, None, None] * dist[None]).astype(np.float32) * np.float32(LOG2E)
    bias = bias.reshape(N_KV_HEADS, PAIRS_PER_KV, 2, BLOCK, 2, half)
    return bias.transpose(0, 2, 4, 3, 1, 5).reshape(N_KV_HEADS, 2, 2, BLOCK, PAIRS_PER_KV * half)


@jax.jit
def kernel(x, norm_in, w_in, conv_w, attn_sinks, norm_conv_out, norm_attn_out, w_out, norm_final):
    bsz, seq, d = x.shape
    assert d == D_MODEL and seq % SEQ_TILE == 0 and SEQ_TILE % BLOCK == 0
    assert w_in.shape == (1, D_MODEL, D_IN_PROJ) and w_out.shape == (1, D_MIX, D_MODEL)
    ts = SEQ_TILE
    const = lambda shape: pl.BlockSpec(shape, lambda b, j: (0,) * len(shape),
                                       pipeline_mode=pl.Buffered(1))
    return pl.pallas_call(
        _layer_kernel,
        grid=(bsz, seq // ts),
        in_specs=[
            pl.BlockSpec((None, ts, D_MODEL), lambda b, j: (b, j, 0)),
            const((1, D_MODEL)),
            const((D_MODEL, D_IN_PROJ)),
            const((CONV_WIDTH, 1, D_CONV)),
            pl.BlockSpec(memory_space=pltpu.SMEM),
            const((1, D_CONV)),
            const((1, D_ATTN)),
            const((D_MIX, D_MODEL)),
            const((1, D_MODEL)),
            const((N_KV_HEADS, 2, 2, BLOCK, PAIRS_PER_KV * BLOCK // 2)),
        ],
        out_specs=pl.BlockSpec((None, ts, D_MODEL), lambda b, j: (b, j, 0)),
        out_shape=jax.ShapeDtypeStruct(x.shape, x.dtype),
        scratch_shapes=[
            pltpu.VMEM((SUBLANES + ts, D_CONV), jnp.float32),
            pltpu.VMEM((KV_VARIANTS, BLOCK + ts, LANES), jnp.bfloat16),
            pltpu.VMEM((KV_VARIANTS, LANES, BLOCK + ts), jnp.bfloat16),
            pltpu.VMEM((ts, D_ATTN), jnp.bfloat16),
            pltpu.VMEM((ts, D_ATTN), jnp.float32),
            pltpu.VMEM((ts, D_CONV), jnp.float32),
            pltpu.VMEM((ts, D_ATTN), jnp.float32),
        ],
        compiler_params=pltpu.CompilerParams(
            dimension_semantics=("arbitrary", "arbitrary"),
            vmem_limit_bytes=VMEM_LIMIT_BYTES),
        name="hybrid_layer",
    )(x, norm_in, w_in[0], conv_w.reshape(CONV_WIDTH, 1, D_CONV), attn_sinks, norm_conv_out, norm_attn_out,
      w_out[0], norm_final[None, :], _distance_bias())
```

```python
import functools

import jax
import jax.numpy as jnp
import numpy as np
from jax import lax
from jax.experimental import pallas as pl
from jax.experimental.pallas import tpu as pltpu

D_MODEL = 1024
D_CONV = 1024
CONV_WIDTH = 3
N_Q_HEADS = 16
N_KV_HEADS = 2
HEAD_DIM = 64
Q_PER_KV = N_Q_HEADS // N_KV_HEADS
D_ATTN = N_Q_HEADS * HEAD_DIM
D_KV = N_KV_HEADS * HEAD_DIM
BLOCK = 128
D_MIX = D_CONV + D_ATTN
D_IN_PROJ = 4 * D_CONV + 2 * D_ATTN + 2 * D_KV
RMS_EPS = 1e-5
LOG2E = 1.4426950408889634

OFF_CB, OFF_CC, OFF_CU, OFF_GC = 0, D_CONV, 2 * D_CONV, 3 * D_CONV
OFF_Q = 4 * D_CONV
OFF_KV = OFF_Q + D_ATTN
OFF_GA = OFF_KV + 2 * D_KV

LANES = 128
SUBLANES = 8
PAIRS_PER_KV = Q_PER_KV * HEAD_DIM // LANES
KV_VARIANTS = 2 * N_KV_HEADS
SEQ_TILE = 512
CONV_CHUNK = 256
VMEM_LIMIT_BYTES = 63 * 1024 * 1024

_NT = (((1,), (1,)), ((), ()))


def _rms(x, gain):
    return x * lax.rsqrt(jnp.mean(x * x, axis=-1, keepdims=True) + RMS_EPS) * gain


def _silu(x):
    return x * (1.0 / (1.0 + jnp.exp2(x * -LOG2E)))


def _after(value, anchor):
    return jnp.where(anchor > jnp.inf, jnp.zeros_like(value), value)


def _mixed_dot(lhs, rhs_f32):
    return lax.dot_general(lhs, rhs_f32, (((1,), (0,)), ((), ())), preferred_element_type=jnp.float32)


def _previous_tile_output(wout_ref, gfin_ref, xbuf, rbuf, attn_buf, cbuf):
    o_a = _mixed_dot(attn_buf[...], wout_ref[D_CONV:D_MIX, :])
    o_c = _mixed_dot(cbuf[...], wout_ref[0:D_CONV, :])
    return _rms(xbuf[...] + rbuf[0] * o_a + rbuf[1] * o_c, gfin_ref[...])


def _layer_kernel(x_ref, gin_ref, win_ref, convw_ref, sinks_ref, gconv_ref, gattn_ref,
                  wout_ref, gfin_ref, bias_ref, o_ref, xbuf, rbuf, ubuf, kbuf, vtbuf, qbuf, attn_buf, cbuf, sgabuf,
                  *, n_tiles, tiles_per_seq):
    t = pl.program_id(0)
    previous_tile = (wout_ref, gfin_ref, xbuf, rbuf, attn_buf, cbuf)

    @pl.when(t == 0)
    def _():
        for buf in (xbuf, rbuf, attn_buf, cbuf):
            buf[...] = jnp.zeros(buf.shape, buf.dtype)

    @pl.when(t < n_tiles)
    def _():
        _tile_body(t % tiles_per_seq, x_ref, gin_ref, win_ref, convw_ref, sinks_ref, gconv_ref, gattn_ref,
                   bias_ref, o_ref, previous_tile, ubuf, kbuf, vtbuf, qbuf, sgabuf)

    @pl.when(t == n_tiles)
    def _():
        o_ref[...] = _previous_tile_output(*previous_tile)


def _tile_body(j, x_ref, gin_ref, win_ref, convw_ref, sinks_ref, gconv_ref, gattn_ref,
               bias_ref, o_ref, previous_tile, ubuf, kbuf, vtbuf, qbuf, sgabuf):
    _, _, xbuf, rbuf, attn_buf, cbuf = previous_tile
    ts = x_ref.shape[0]
    f32, bf16 = jnp.float32, jnp.bfloat16

    @pl.when(j == 0)
    def _():
        ubuf[0:SUBLANES, :] = jnp.zeros((SUBLANES, D_CONV), f32)
        kbuf[:, 0:BLOCK, :] = jnp.zeros((KV_VARIANTS, BLOCK, LANES), bf16)
        vtbuf[:, :, 0:BLOCK] = jnp.zeros((KV_VARIANTS, LANES, BLOCK), bf16)

    out_previous = _previous_tile_output(*previous_tile)
    o_ref[...] = out_previous

    x = x_ref[...]
    xg = x * gin_ref[...]
    qkv_unscaled = _mixed_dot(xg.astype(bf16), win_ref[:, OFF_Q:OFF_GA])
    q_unscaled = qkv_unscaled[:, :D_ATTN]
    r_in = lax.rsqrt(jnp.mean(x * x, axis=-1, keepdims=True) + RMS_EPS)
    h = (xg * r_in).astype(bf16)

    def proj(off, width):
        return _mixed_dot(h, win_ref[:, off:off + width])

    qbuf[...] = _after(q_unscaled * (r_in * (HEAD_DIM ** -0.5 * LOG2E)), out_previous).astype(bf16)
    kv = qkv_unscaled[:, D_ATTN:] * r_in
    k = kv[:, :D_KV]
    lo = lax.broadcasted_iota(jnp.int32, (ts, LANES), 1) < HEAD_DIM
    k_swapped = pltpu.roll(k, HEAD_DIM, axis=1)
    zero = jnp.zeros_like(k)
    for i, kk in enumerate([jnp.where(lo, k, zero), jnp.where(lo, zero, k_swapped),
                            jnp.where(lo, k_swapped, zero), jnp.where(lo, zero, k)]):
        kbuf[i, BLOCK:BLOCK + ts, :] = kk.astype(bf16)
    vt = kv[:, D_KV:].T.astype(bf16)
    vt0, vt1 = vt[:HEAD_DIM, :], vt[HEAD_DIM:, :]
    zrows = jnp.zeros_like(vt0)
    for i, vv in enumerate([(vt0, zrows), (zrows, vt0), (vt1, zrows), (zrows, vt1)]):
        vtbuf[i, :, BLOCK:BLOCK + ts] = jnp.concatenate(vv, axis=0)

    HALF = BLOCK // 2
    half_lanes = PAIRS_PER_KV * HALF
    kk = lax.broadcasted_iota(jnp.int32, (HALF, half_lanes), 0)
    qq = lax.broadcasted_iota(jnp.int32, (HALF, half_lanes), 1) & (HALF - 1)
    tri = kk <= qq
    lane_in_half = lax.broadcasted_iota(jnp.int32, (1, half_lanes), 1)
    neg = jnp.where(j == 0, -jnp.inf, 0.0).astype(f32)

    chunks = [(qb, g) for qb in range(ts // BLOCK) for g in range(N_KV_HEADS)]

    def scores(chunk):
        qb, g = chunk
        r0, t0 = qb * BLOCK, g * PAIRS_PER_KV
        out = []
        for half in range(2):
            q0 = r0 + half * HALF
            qs = jnp.concatenate(
                [qbuf[q0:q0 + HALF, (t0 + jp) * LANES:(t0 + jp + 1) * LANES] for jp in range(PAIRS_PER_KV)],
                axis=0)
            k0 = r0 + half * HALF
            kcat = jnp.concatenate([kbuf[2 * g, k0:k0 + 3 * HALF, :],
                                    kbuf[2 * g + 1, k0:k0 + 3 * HALF, :]], axis=0)
            out.append(lax.dot_general(kcat, qs, _NT, preferred_element_type=f32))
        return out

    def softmax(chunk, s_halves):
        qb, g = chunk
        p_cols, inv_sums = [], [[None, None], [None, None]]
        for half, s_all in enumerate(s_halves):
            p_rows = []
            for par in range(2):
                base = par * 3 * HALF
                if half == 0:
                    prev_lo, prev_hi = s_all[base:base + HALF], s_all[base + HALF:base + 2 * HALF]
                    cur_lo = s_all[base + 2 * HALF:base + 3 * HALF]
                    if qb == 0:
                        prev_lo, prev_hi = prev_lo + neg, prev_hi + neg
                    s = jnp.concatenate([jnp.where(tri, cur_lo, prev_lo), prev_hi], axis=0)
                else:
                    prev_hi = s_all[base:base + HALF]
                    cur_lo, cur_hi = s_all[base + HALF:base + 2 * HALF], s_all[base + 2 * HALF:base + 3 * HALF]
                    if qb == 0:
                        prev_hi = prev_hi + neg
                    s = jnp.concatenate([cur_lo, jnp.where(tri, cur_hi, prev_hi)], axis=0)
                s = s + bias_ref[g, par, half]
                sink = jnp.full((1, half_lanes), sinks_ref[0, g * Q_PER_KV + par] * LOG2E, f32)
                for jp in range(1, PAIRS_PER_KV):
                    sink = jnp.where(lane_in_half >= jp * HALF, sinks_ref[0, g * Q_PER_KV + 2 * jp + par] * LOG2E, sink)
                m = jnp.maximum(jnp.max(s, axis=0, keepdims=True), sink)
                p = jnp.exp2(s - m)
                inv_sums[par][half] = 1.0 / (jnp.sum(p, axis=0, keepdims=True) + jnp.exp2(sink - m))
                p_lo, p_hi = p[:HALF], p[HALF:]
                zero_p = jnp.zeros_like(p_lo)
                if half == 0:
                    p_prev = [jnp.where(tri, zero_p, p_lo), p_hi]
                    p_cur = [jnp.where(tri, p_lo, zero_p), zero_p]
                else:
                    p_prev = [zero_p, jnp.where(tri, zero_p, p_hi)]
                    p_cur = [p_lo, jnp.where(tri, p_hi, zero_p)]
                p_rows += p_prev + p_cur
            p_cols.append(jnp.concatenate(p_rows, axis=0).astype(bf16))
        inv = [jnp.concatenate(inv_sums[par], axis=1) for par in range(2)]
        return jnp.concatenate(p_cols, axis=1), inv

    def weighted_values(chunk, p_all, inv_sums):
        qb, g = chunk
        r0, t0 = qb * BLOCK, g * PAIRS_PER_KV
        vtcat = jnp.concatenate([vtbuf[2 * g, :, r0:r0 + 2 * BLOCK],
                                 vtbuf[2 * g + 1, :, r0:r0 + 2 * BLOCK]], axis=1)
        o_all = jnp.dot(vtcat, p_all, preferred_element_type=f32)
        scale = jnp.concatenate([jnp.broadcast_to(r, (HEAD_DIM, r.shape[1])) for r in inv_sums], axis=0)
        o_all = o_all * scale
        for half in range(2):
            o_half = o_all[:, half * half_lanes:(half + 1) * half_lanes].T
            for jp in range(PAIRS_PER_KV):
                attn_buf[r0 + half * HALF:r0 + (half + 1) * HALF, (t0 + jp) * LANES:(t0 + jp + 1) * LANES] = \
                    o_half[jp * HALF:(jp + 1) * HALF, :]

    def conv_elementwise(c, cc, cu, cb, gc):
        cs = slice(c * CONV_CHUNK, (c + 1) * CONV_CHUNK)
        u = cc * cu
        ubuf[SUBLANES:SUBLANES + ts, cs] = u
        u1 = ubuf[SUBLANES - 1:SUBLANES - 1 + ts, cs]
        u2 = ubuf[SUBLANES - 2:SUBLANES - 2 + ts, cs]
        ubuf[0:SUBLANES, cs] = ubuf[ts:ts + SUBLANES, cs]
        y = cb * (convw_ref[2, :, cs] * u + convw_ref[1, :, cs] * u1 + convw_ref[0, :, cs] * u2)
        cbuf[:, cs] = y * gconv_ref[:, cs] * _silu(gc)
        return jnp.sum(y * y, axis=-1, keepdims=True)

    n_conv = D_CONV // CONV_CHUNK
    assert len(chunks) == 2 * n_conv
    sgabuf[...] = _silu(proj(OFF_GA, D_ATTN))
    s_ahead = [scores(chunks[0]), scores(chunks[1])]
    ssq_c = jnp.zeros((ts, 1), f32)
    previous = None
    for c in range(n_conv):
        projected = []
        for i, off in ((2 * c, OFF_CC), (2 * c + 1, OFF_CU)):
            projected.append(proj(off + c * CONV_CHUNK, CONV_CHUNK))
            weighted_values(chunks[i], *softmax(chunks[i], s_ahead.pop(0)))
            if i + 2 < len(chunks):
                s_ahead.append(scores(chunks[i + 2]))
        projected.append(proj(OFF_CB + c * CONV_CHUNK, CONV_CHUNK))
        if previous is not None:
            ssq_c = ssq_c + conv_elementwise(c - 1, *previous)
        projected.append(proj(OFF_GC + c * CONV_CHUNK, CONV_CHUNK))
        previous = projected

    kbuf[:, 0:BLOCK, :] = kbuf[:, ts:ts + BLOCK, :]
    vtbuf[:, :, 0:BLOCK] = vtbuf[:, :, ts:ts + BLOCK]

    attn = attn_buf[...]
    rbuf[0] = lax.rsqrt(jnp.mean(attn * attn, axis=-1, keepdims=True) + RMS_EPS)
    attn_buf[...] = attn * gattn_ref[...] * sgabuf[...]
    ssq_c = ssq_c + conv_elementwise(n_conv - 1, *previous)
    rbuf[1] = lax.rsqrt(ssq_c * (1.0 / D_CONV) + RMS_EPS)
    xbuf[...] = x


def _distance_bias():
    half = BLOCK // 2
    slopes = np.exp2(-8.0 * np.arange(1, N_Q_HEADS + 1, dtype=np.float32) / N_Q_HEADS).astype(np.float32)
    key = np.arange(BLOCK)[:, None]
    query = np.arange(BLOCK)[None, :]
    dist = np.where(key <= query, query - key, BLOCK + query - key).astype(np.float32)
    bias = (-slopes[:, None, None] * dist[None]).astype(np.float32) * np.float32(LOG2E)
    bias = bias.reshape(N_KV_HEADS, PAIRS_PER_KV, 2, BLOCK, 2, half)
    return bias.transpose(0, 2, 4, 3, 1, 5).reshape(N_KV_HEADS, 2, 2, BLOCK, PAIRS_PER_KV * half)


@jax.jit
def kernel(x, norm_in, w_in, conv_w, attn_sinks, norm_conv_out, norm_attn_out, w_out, norm_final):
    bsz, seq, d = x.shape
    assert d == D_MODEL and seq % SEQ_TILE == 0 and SEQ_TILE % BLOCK == 0
    assert w_in.shape == (1, D_MODEL, D_IN_PROJ) and w_out.shape == (1, D_MIX, D_MODEL)
    ts = SEQ_TILE
    tiles_per_seq = seq // ts
    n_tiles = bsz * tiles_per_seq

    def x_tile(shift):
        def index_map(t):
            i = jnp.clip(t + shift, 0, n_tiles - 1)
            return (i // tiles_per_seq, i % tiles_per_seq, 0)
        return pl.BlockSpec((None, ts, D_MODEL), index_map)

    const = lambda shape: pl.BlockSpec(shape, lambda t: (0,) * len(shape), pipeline_mode=pl.Buffered(1))
    return pl.pallas_call(
        functools.partial(_layer_kernel, n_tiles=n_tiles, tiles_per_seq=tiles_per_seq),
        grid=(n_tiles + 1,),
        in_specs=[
            x_tile(0),
            const((1, D_MODEL)),
            const((D_MODEL, D_IN_PROJ)),
            const((CONV_WIDTH, 1, D_CONV)),
            pl.BlockSpec(memory_space=pltpu.SMEM),
            const((1, D_CONV)),
            const((1, D_ATTN)),
            const((D_MIX, D_MODEL)),
            const((1, D_MODEL)),
            const((N_KV_HEADS, 2, 2, BLOCK, PAIRS_PER_KV * BLOCK // 2)),
        ],
        out_specs=x_tile(-1),
        out_shape=jax.ShapeDtypeStruct(x.shape, x.dtype),
        scratch_shapes=[
            pltpu.VMEM((ts, D_MODEL), jnp.float32),
            pltpu.VMEM((2, ts, 1), jnp.float32),
            pltpu.VMEM((SUBLANES + ts, D_CONV), jnp.float32),
            pltpu.VMEM((KV_VARIANTS, BLOCK + ts, LANES), jnp.bfloat16),
            pltpu.VMEM((KV_VARIANTS, LANES, BLOCK + ts), jnp.bfloat16),
            pltpu.VMEM((ts, D_ATTN), jnp.bfloat16),
            pltpu.VMEM((ts, D_ATTN), jnp.float32),
            pltpu.VMEM((ts, D_CONV), jnp.float32),
            pltpu.VMEM((ts, D_ATTN), jnp.float32),
        ],
        compiler_params=pltpu.CompilerParams(
            dimension_semantics=("arbitrary",),
            vmem_limit_bytes=VMEM_LIMIT_BYTES),
        name="hybrid_layer",
    )(x, norm_in, w_in[0], conv_w.reshape(CONV_WIDTH, 1, D_CONV), attn_sinks, norm_conv_out, norm_attn_out,
      w_out[0], norm_final[None, :], _distance_bias())
```

```python
import jax
import jax.numpy as jnp
import numpy as np
from jax import lax
from jax.experimental import pallas as pl
from jax.experimental.pallas import tpu as pltpu

D_MODEL = 1024
D_CONV = 1024
CONV_WIDTH = 3
N_Q_HEADS = 16
N_KV_HEADS = 2
HEAD_DIM = 64
Q_PER_KV = N_Q_HEADS // N_KV_HEADS
D_ATTN = N_Q_HEADS * HEAD_DIM
D_KV = N_KV_HEADS * HEAD_DIM
BLOCK = 128
D_MIX = D_CONV + D_ATTN
D_IN_PROJ = 4 * D_CONV + 2 * D_ATTN + 2 * D_KV
RMS_EPS = 1e-5
LOG2E = 1.4426950408889634

OFF_CB, OFF_CC, OFF_CU, OFF_GC = 0, D_CONV, 2 * D_CONV, 3 * D_CONV
OFF_Q = 4 * D_CONV
OFF_KV = OFF_Q + D_ATTN
OFF_GA = OFF_KV + 2 * D_KV

LANES = 128
SUBLANES = 8
PAIRS_PER_KV = Q_PER_KV * HEAD_DIM // LANES
KV_VARIANTS = 2 * N_KV_HEADS
SEQ_TILE = 512
CONV_CHUNK = 256
VMEM_LIMIT_BYTES = 60 * 1024 * 1024

_NT = (((1,), (1,)), ((), ()))


def _rms(x, gain):
    return x * lax.rsqrt(jnp.mean(x * x, axis=-1, keepdims=True) + RMS_EPS) * gain


def _silu(x):
    return x * (1.0 / (1.0 + jnp.exp2(x * -LOG2E)))


def _layer_kernel(x_ref, gin_ref, win_ref, convw_ref, sinks_ref, gconv_ref, gattn_ref,
                  wout_ref, gfin_ref, bias_ref, o_ref, ubuf, kbuf, vtbuf, qbuf, attn_buf, cbuf, sgabuf):
    ts = x_ref.shape[0]
    j = pl.program_id(1)
    f32, bf16 = jnp.float32, jnp.bfloat16

    @pl.when(j == 0)
    def _():
        ubuf[0:SUBLANES, :] = jnp.zeros((SUBLANES, D_CONV), f32)
        kbuf[:, 0:BLOCK, :] = jnp.zeros((KV_VARIANTS, BLOCK, LANES), bf16)
        vtbuf[:, :, 0:BLOCK] = jnp.zeros((KV_VARIANTS, LANES, BLOCK), bf16)

    def mixed_dot(lhs, rhs_f32):
        return lax.dot_general(lhs, rhs_f32, (((1,), (0,)), ((), ())), preferred_element_type=f32)

    x = x_ref[...]
    xg = x * gin_ref[...]
    qkv_unscaled = mixed_dot(xg.astype(bf16), win_ref[:, OFF_Q:OFF_GA])
    q_unscaled = qkv_unscaled[:, :D_ATTN]
    r_in = lax.rsqrt(jnp.mean(x * x, axis=-1, keepdims=True) + RMS_EPS)
    h = (xg * r_in).astype(bf16)

    def proj(off, width):
        return mixed_dot(h, win_ref[:, off:off + width])

    qbuf[...] = (q_unscaled * (r_in * (HEAD_DIM ** -0.5 * LOG2E))).astype(bf16)
    kv = qkv_unscaled[:, D_ATTN:] * r_in
    k = kv[:, :D_KV]
    lo = lax.broadcasted_iota(jnp.int32, (ts, LANES), 1) < HEAD_DIM
    k_swapped = pltpu.roll(k, HEAD_DIM, axis=1)
    zero = jnp.zeros_like(k)
    for i, kk in enumerate([jnp.where(lo, k, zero), jnp.where(lo, zero, k_swapped),
                            jnp.where(lo, k_swapped, zero), jnp.where(lo, zero, k)]):
        kbuf[i, BLOCK:BLOCK + ts, :] = kk.astype(bf16)
    vt = kv[:, D_KV:].T.astype(bf16)
    vt0, vt1 = vt[:HEAD_DIM, :], vt[HEAD_DIM:, :]
    zrows = jnp.zeros_like(vt0)
    for i, vv in enumerate([(vt0, zrows), (zrows, vt0), (vt1, zrows), (zrows, vt1)]):
        vtbuf[i, :, BLOCK:BLOCK + ts] = jnp.concatenate(vv, axis=0)

    HALF = BLOCK // 2
    half_lanes = PAIRS_PER_KV * HALF
    kk = lax.broadcasted_iota(jnp.int32, (HALF, half_lanes), 0)
    qq = lax.broadcasted_iota(jnp.int32, (HALF, half_lanes), 1) & (HALF - 1)
    tri = kk <= qq
    lane_in_half = lax.broadcasted_iota(jnp.int32, (1, half_lanes), 1)
    neg = jnp.where(j == 0, -jnp.inf, 0.0).astype(f32)

    chunks = [(qb, g) for qb in range(ts // BLOCK) for g in range(N_KV_HEADS)]

    def scores(chunk):
        qb, g = chunk
        r0, t0 = qb * BLOCK, g * PAIRS_PER_KV
        out = []
        for half in range(2):
            q0 = r0 + half * HALF
            qs = jnp.concatenate(
                [qbuf[q0:q0 + HALF, (t0 + jp) * LANES:(t0 + jp + 1) * LANES] for jp in range(PAIRS_PER_KV)],
                axis=0)
            k0 = r0 + half * HALF
            kcat = jnp.concatenate([kbuf[2 * g, k0:k0 + 3 * HALF, :],
                                    kbuf[2 * g + 1, k0:k0 + 3 * HALF, :]], axis=0)
            out.append(lax.dot_general(kcat, qs, _NT, preferred_element_type=f32))
        return out

    def softmax(chunk, s_halves):
        qb, g = chunk
        p_cols, inv_sums = [], [[None, None], [None, None]]
        for half, s_all in enumerate(s_halves):
            p_rows = []
            for par in range(2):
                base = par * 3 * HALF
                if half == 0:
                    prev_lo, prev_hi = s_all[base:base + HALF], s_all[base + HALF:base + 2 * HALF]
                    cur_lo = s_all[base + 2 * HALF:base + 3 * HALF]
                    if qb == 0:
                        prev_lo, prev_hi = prev_lo + neg, prev_hi + neg
                    s = jnp.concatenate([jnp.where(tri, cur_lo, prev_lo), prev_hi], axis=0)
                else:
                    prev_hi = s_all[base:base + HALF]
                    cur_lo, cur_hi = s_all[base + HALF:base + 2 * HALF], s_all[base + 2 * HALF:base + 3 * HALF]
                    if qb == 0:
                        prev_hi = prev_hi + neg
                    s = jnp.concatenate([cur_lo, jnp.where(tri, cur_hi, prev_hi)], axis=0)
                s = s + bias_ref[g, par, half]
                sink = jnp.full((1, half_lanes), sinks_ref[0, g * Q_PER_KV + par] * LOG2E, f32)
                for jp in range(1, PAIRS_PER_KV):
                    sink = jnp.where(lane_in_half >= jp * HALF, sinks_ref[0, g * Q_PER_KV + 2 * jp + par] * LOG2E, sink)
                m = jnp.maximum(jnp.max(s, axis=0, keepdims=True), sink)
                p = jnp.exp2(s - m)
                inv_sums[par][half] = 1.0 / (jnp.sum(p, axis=0, keepdims=True) + jnp.exp2(sink - m))
                p_lo, p_hi = p[:HALF], p[HALF:]
                zero_p = jnp.zeros_like(p_lo)
                if half == 0:
                    p_prev = [jnp.where(tri, zero_p, p_lo), p_hi]
                    p_cur = [jnp.where(tri, p_lo, zero_p), zero_p]
                else:
                    p_prev = [zero_p, jnp.where(tri, zero_p, p_hi)]
                    p_cur = [p_lo, jnp.where(tri, p_hi, zero_p)]
                p_rows += p_prev + p_cur
            p_cols.append(jnp.concatenate(p_rows, axis=0).astype(bf16))
        inv = [jnp.concatenate(inv_sums[par], axis=1) for par in range(2)]
        return jnp.concatenate(p_cols, axis=1), inv

    def weighted_values(chunk, p_all, inv_sums):
        qb, g = chunk
        r0, t0 = qb * BLOCK, g * PAIRS_PER_KV
        vtcat = jnp.concatenate([vtbuf[2 * g, :, r0:r0 + 2 * BLOCK],
                                 vtbuf[2 * g + 1, :, r0:r0 + 2 * BLOCK]], axis=1)
        o_all = jnp.dot(vtcat, p_all, preferred_element_type=f32)
        scale = jnp.concatenate([jnp.broadcast_to(r, (HEAD_DIM, r.shape[1])) for r in inv_sums], axis=0)
        o_all = o_all * scale
        for half in range(2):
            o_half = o_all[:, half * half_lanes:(half + 1) * half_lanes].T
            for jp in range(PAIRS_PER_KV):
                attn_buf[r0 + half * HALF:r0 + (half + 1) * HALF, (t0 + jp) * LANES:(t0 + jp + 1) * LANES] = \
                    o_half[jp * HALF:(jp + 1) * HALF, :]

    def conv_elementwise(c, cc, cu, cb, gc):
        cs = slice(c * CONV_CHUNK, (c + 1) * CONV_CHUNK)
        u = cc * cu
        ubuf[SUBLANES:SUBLANES + ts, cs] = u
        u1 = ubuf[SUBLANES - 1:SUBLANES - 1 + ts, cs]
        u2 = ubuf[SUBLANES - 2:SUBLANES - 2 + ts, cs]
        ubuf[0:SUBLANES, cs] = ubuf[ts:ts + SUBLANES, cs]
        y = cb * (convw_ref[2, :, cs] * u + convw_ref[1, :, cs] * u1 + convw_ref[0, :, cs] * u2)
        cbuf[:, cs] = (y * gconv_ref[:, cs] * _silu(gc)).astype(bf16)
        return jnp.sum(y * y, axis=-1, keepdims=True)

    n_conv = D_CONV // CONV_CHUNK
    assert len(chunks) == 2 * n_conv
    sgabuf[...] = _silu(proj(OFF_GA, D_ATTN))
    s_ahead = [scores(chunks[0]), scores(chunks[1])]
    ssq_c = jnp.zeros((ts, 1), f32)
    previous = None
    for c in range(n_conv):
        projected = []
        for i, off in ((2 * c, OFF_CC), (2 * c + 1, OFF_CU)):
            projected.append(proj(off + c * CONV_CHUNK, CONV_CHUNK))
            weighted_values(chunks[i], *softmax(chunks[i], s_ahead.pop(0)))
            if i + 2 < len(chunks):
                s_ahead.append(scores(chunks[i + 2]))
        projected.append(proj(OFF_CB + c * CONV_CHUNK, CONV_CHUNK))
        if previous is not None:
            ssq_c = ssq_c + conv_elementwise(c - 1, *previous)
        projected.append(proj(OFF_GC + c * CONV_CHUNK, CONV_CHUNK))
        previous = projected

    kbuf[:, 0:BLOCK, :] = kbuf[:, ts:ts + BLOCK, :]
    vtbuf[:, :, 0:BLOCK] = vtbuf[:, :, ts:ts + BLOCK]

    attn = attn_buf[...]
    attn_y = (attn * gattn_ref[...] * sgabuf[...]).astype(bf16)
    split = (n_conv - 1) * CONV_CHUNK
    o_c = mixed_dot(cbuf[:, 0:split], wout_ref[0:split, :])
    ssq_c = ssq_c + conv_elementwise(n_conv - 1, *previous)
    o_a = mixed_dot(attn_y, wout_ref[D_CONV:D_MIX, :])
    o_c = o_c + mixed_dot(cbuf[:, split:D_CONV], wout_ref[split:D_CONV, :])
    r_a = lax.rsqrt(jnp.mean(attn * attn, axis=-1, keepdims=True) + RMS_EPS)
    r_c = lax.rsqrt(ssq_c * (1.0 / D_CONV) + RMS_EPS)
    o_ref[...] = _rms(x + r_a * o_a + r_c * o_c, gfin_ref[...])


def _distance_bias():
    half = BLOCK // 2
    slopes = np.exp2(-8.0 * np.arange(1, N_Q_HEADS + 1, dtype=np.float32) / N_Q_HEADS).astype(np.float32)
    key = np.arange(BLOCK)[:, None]
    query = np.arange(BLOCK)[None, :]
    dist = np.where(key <= query, query - key, BLOCK + query - key).astype(np.float32)
    bias = (-slopes[:, None, None] * dist[None]).astype(np.float32) * np.float32(LOG2E)
    bias = bias.reshape(N_KV_HEADS, PAIRS_PER_KV, 2, BLOCK, 2, half)
    return bias.transpose(0, 2, 4, 3, 1, 5).reshape(N_KV_HEADS, 2, 2, BLOCK, PAIRS_PER_KV * half)


@jax.jit
def kernel(x, norm_in, w_in, conv_w, attn_sinks, norm_conv_out, norm_attn_out, w_out, norm_final):
    bsz, seq, d = x.shape
    assert d == D_MODEL and seq % SEQ_TILE == 0 and SEQ_TILE % BLOCK == 0
    assert w_in.shape == (1, D_MODEL, D_IN_PROJ) and w_out.shape == (1, D_MIX, D_MODEL)
    ts = SEQ_TILE
    const = lambda shape: pl.BlockSpec(shape, lambda b, j: (0,) * len(shape),
                                       pipeline_mode=pl.Buffered(1))
    return pl.pallas_call(
        _layer_kernel,
        grid=(bsz, seq // ts),
        in_specs=[
            pl.BlockSpec((None, ts, D_MODEL), lambda b, j: (b, j, 0)),
            const((1, D_MODEL)),
            const((D_MODEL, D_IN_PROJ)),
            const((CONV_WIDTH, 1, D_CONV)),
            pl.BlockSpec(memory_space=pltpu.SMEM),
            const((1, D_CONV)),
            const((1, D_ATTN)),
            const((D_MIX, D_MODEL)),
            const((1, D_MODEL)),
            const((N_KV_HEADS, 2, 2, BLOCK, PAIRS_PER_KV * BLOCK // 2)),
        ],
        out_specs=pl.BlockSpec((None, ts, D_MODEL), lambda b, j: (b, j, 0)),
        out_shape=jax.ShapeDtypeStruct(x.shape, x.dtype),
        scratch_shapes=[
            pltpu.VMEM((SUBLANES + ts, D_CONV), jnp.float32),
            pltpu.VMEM((KV_VARIANTS, BLOCK + ts, LANES), jnp.bfloat16),
            pltpu.VMEM((KV_VARIANTS, LANES, BLOCK + ts), jnp.bfloat16),
            pltpu.VMEM((ts, D_ATTN), jnp.bfloat16),
            pltpu.VMEM((ts, D_ATTN), jnp.float32),
            pltpu.VMEM((ts, D_CONV), jnp.bfloat16),
            pltpu.VMEM((ts, D_ATTN), jnp.float32),
        ],
        compiler_params=pltpu.CompilerParams(
            dimension_semantics=("arbitrary", "arbitrary"),
            vmem_limit_bytes=VMEM_LIMIT_BYTES),
        name="hybrid_layer",
    )(x, norm_in, w_in[0], conv_w.reshape(CONV_WIDTH, 1, D_CONV), attn_sinks, norm_conv_out, norm_attn_out,
      w_out[0], norm_final[None, :], _distance_bias())
```
